```python
import functools
import jax, jax.numpy as jnp
from jax import lax
import numpy as np

D_MODEL = 1024
BATCH = 8
SEQ = 2048
DEPTH = 2
DEC_BATCH = 128
DEC_SEQ = 4
PAST_LEN = 2048
PAGE_SIZE = 128

SSD_EXPAND = 2
D_INNER = SSD_EXPAND * D_MODEL
SSD_HEADDIM = 64
SSD_HEADS = D_INNER // SSD_HEADDIM
SSD_GROUPS = 4
HEADS_PER_GROUP = SSD_HEADS // SSD_GROUPS
D_STATE = 128
CONV_W = 4
CONV_DIM = D_INNER + 2 * SSD_GROUPS * D_STATE
SSD_CHUNK = 128
ATT_HEADS = 8
ATT_HEAD_DIM = 128
ATT_WIDTH = ATT_HEADS * ATT_HEAD_DIM
MOBA_BLOCK = 256
MOBA_TOPK = 3
MOBA_Q_CHUNK = 32
D_FF = 2816
FFN_SCALE = 0.5
EPS = 1e-6
IN_SIZES = (D_INNER, CONV_DIM, SSD_HEADS, ATT_WIDTH, ATT_WIDTH, ATT_WIDTH, D_MODEL, D_MODEL)
IN_COLS = D_INNER + CONV_DIM + SSD_HEADS + 3 * ATT_WIDTH + 2 * D_MODEL

kernel_name = "hybrid_ssd_moba_macaron_step"


def rmsnorm(x, g):
    x32 = x.astype(jnp.float32)
    y = x32 * lax.rsqrt(jnp.mean(x32 * x32, axis=-1, keepdims=True) + EPS)
    return (y * g.astype(jnp.float32)).astype(x.dtype)


def half_step_ffn(x, g, w_in, w_out):
    a, b = jnp.split(rmsnorm(x, g) @ w_in, 2, axis=-1)
    return x + FFN_SCALE * ((jax.nn.silu(a) * b) @ w_out)


def gated_group_rmsnorm(y, z, g):
    u = y.astype(jnp.float32) * jax.nn.silu(z.astype(jnp.float32))
    shp = u.shape
    u = u.reshape(shp[:-1] + (SSD_GROUPS, D_INNER // SSD_GROUPS))
    u = u * lax.rsqrt(jnp.mean(u * u, axis=-1, keepdims=True) + EPS)
    return (u.reshape(shp) * g.astype(jnp.float32)).astype(y.dtype)


def causal_depthwise_conv(u_ext, w, b):
    out = lax.conv_general_dilated(
        u_ext, w[:, None, :].astype(u_ext.dtype), window_strides=(1,), padding="VALID",
        dimension_numbers=("NWC", "WIO", "NWC"), feature_group_count=u_ext.shape[-1])
    return out + b


def ssd_scan(xh, dt, a, bm, cm, h0):
    f32 = jnp.float32
    b, L = xh.shape[:2]
    q = min(SSD_CHUNK, L)
    nc = -(-L // q)
    pad = nc * q - L
    padl = lambda t: jnp.pad(t, [(0, 0), (0, pad)] + [(0, 0)] * (t.ndim - 2))
    x = padl(xh.astype(f32) * dt[..., None]).reshape(b, nc, q, SSD_GROUPS, HEADS_PER_GROUP, SSD_HEADDIM)
    da = padl(dt * a).reshape(b, nc, q, SSD_GROUPS, HEADS_PER_GROUP)
    bc = padl(bm.astype(f32)).reshape(b, nc, q, SSD_GROUPS, D_STATE)
    cc = padl(cm.astype(f32)).reshape(b, nc, q, SSD_GROUPS, D_STATE)
    a_cum = jnp.cumsum(da, axis=2)
    causal = jnp.tril(jnp.ones((q, q), bool))[None, None, :, :, None, None]
    decay_in = jnp.exp(jnp.where(causal, a_cum[:, :, :, None] - a_cum[:, :, None, :], -jnp.inf))
    cb = jnp.einsum("bclgn,bcsgn->bclsg", cc, bc)
    y_diag = jnp.einsum("bclsgh,bcsghp->bclghp", cb[..., None] * decay_in, x)
    decay_to_end = jnp.exp(a_cum[:, :, -1:] - a_cum)
    chunk_states = jnp.einsum("bclgn,bclghp->bcghpn", bc, x * decay_to_end[..., None])
    h_init = h0.astype(f32).reshape(b, 1, SSD_GROUPS, HEADS_PER_GROUP, SSD_HEADDIM, D_STATE)
    states = jnp.concatenate([h_init, chunk_states], axis=1)
    chunk_cum = jnp.cumsum(jnp.pad(a_cum[:, :, -1], ((0, 0), (1, 0), (0, 0), (0, 0))), axis=1)
    chunk_causal = jnp.tril(jnp.ones((nc + 1, nc + 1), bool))[None, :, :, None, None]
    decay_chunk = jnp.exp(jnp.where(chunk_causal, chunk_cum[:, :, None] - chunk_cum[:, None, :], -jnp.inf))
    carried = jnp.einsum("bzcgh,bcghpn->bzghpn", decay_chunk, states)
    y_off = jnp.einsum("bclgn,bcghpn->bclghp", cc, carried[:, :-1]) * jnp.exp(a_cum)[..., None]
    y = (y_diag + y_off).reshape(b, nc * q, SSD_HEADS, SSD_HEADDIM)[:, :L]
    return y, carried[:, -1].reshape(b, SSD_HEADS, SSD_HEADDIM, D_STATE).astype(h0.dtype)


def ssd_branch(z, xbc, dt_raw, conv_buf, h0, conv_w, conv_b, dt_bias, a_log, d_skip, norm_g):
    b, L, _ = xbc.shape
    xbc_ext = jnp.concatenate([conv_buf.astype(xbc.dtype), xbc], axis=1)
    new_conv = xbc_ext[:, -(CONV_W - 1):]
    u = jax.nn.silu(causal_depthwise_conv(xbc_ext, conv_w, conv_b))
    xs, bm, cm = jnp.split(u, [D_INNER, D_INNER + SSD_GROUPS * D_STATE], axis=-1)
    xs = xs.reshape(b, L, SSD_HEADS, SSD_HEADDIM)
    dt = jax.nn.softplus(dt_raw.astype(jnp.float32) + dt_bias.astype(jnp.float32))
    a = -jnp.exp(a_log.astype(jnp.float32))
    y, h_new = ssd_scan(xs, dt, a, bm.reshape(b, L, SSD_GROUPS, D_STATE),
                        cm.reshape(b, L, SSD_GROUPS, D_STATE), h0)
    y = y + d_skip.astype(jnp.float32)[:, None] * xs.astype(jnp.float32)
    y = y.reshape(b, L, D_INNER).astype(z.dtype)
    return gated_group_rmsnorm(y, z, norm_g), new_conv, h_new


def moba_sequence(q, k, v, q_pos, slopes):
    tk = k.shape[0]
    nb = max(-(-tk // MOBA_BLOCK), MOBA_TOPK)
    pad = nb * MOBA_BLOCK - tk

    def to_blocks(t):
        t = jnp.pad(t, ((0, pad), (0, 0), (0, 0)))
        return t.reshape(nb, MOBA_BLOCK, ATT_HEADS, ATT_HEAD_DIM).transpose(2, 0, 1, 3)

    kb, vb = to_blocks(k), to_blocks(v)
    kmean = jnp.mean(kb.astype(jnp.float32), axis=2)
    lq = q.shape[0]
    qc = min(MOBA_Q_CHUNK, lq)
    nq = -(-lq // qc)
    qpad = nq * qc - lq
    q_c = jnp.pad(q, ((0, qpad), (0, 0), (0, 0))).reshape(nq, qc, ATT_HEADS, ATT_HEAD_DIM)
    p_c = jnp.pad(q_pos, (0, qpad)).reshape(nq, qc)
    head_idx = jnp.arange(ATT_HEADS)[None, :, None]
    scale = ATT_HEAD_DIM ** -0.5

    def attend_chunk(args):
        qch, pch = args
        j = pch // MOBA_BLOCK
        gate = jnp.einsum("qhd,hnd->qhn", qch.astype(jnp.float32), kmean)
        gate = jnp.where(jnp.arange(nb)[None, None, :] < j[:, None, None], gate, -jnp.inf)
        _, top = lax.top_k(gate, MOBA_TOPK)
        sel = jnp.concatenate([top, jnp.broadcast_to(j[:, None, None], (qc, ATT_HEADS, 1)).astype(top.dtype)], axis=-1)
        kg = kb[head_idx, sel]
        vg = vb[head_idx, sel]
        kpos = sel[..., None] * MOBA_BLOCK + jnp.arange(MOBA_BLOCK)
        slot_ok = jnp.concatenate([jnp.arange(MOBA_TOPK)[None, :] < j[:, None],
                                   jnp.ones((qc, 1), bool)], axis=1)
        t = pch[:, None, None, None]
        mask = slot_ok[:, None, :, None] & (kpos <= t)
        s = jnp.einsum("qhd,qhrsd->qhrs", qch, kg, preferred_element_type=jnp.float32) * scale
        s = s - slopes[None, :, None, None] * (t - kpos).astype(jnp.float32)
        s = jnp.where(mask, s, -jnp.inf)
        p = jax.nn.softmax(s.reshape(qc, ATT_HEADS, -1), axis=-1).reshape(s.shape)
        return jnp.einsum("qhrs,qhrsd->qhd", p.astype(vg.dtype), vg)

    out = lax.map(attend_chunk, (q_c, p_c))
    return out.reshape(nq * qc, ATT_HEADS, ATT_HEAD_DIM)[:lq]


def moba_prompt(q, k, v, slopes):
    q_pos = jnp.arange(q.shape[1], dtype=jnp.int32)
    return lax.map(lambda a: moba_sequence(a[0], a[1], a[2], q_pos, slopes), (q, k, v))


def moba_sample(q, k, v, pool_k, pool_v, page_table, layer, slopes):
    past_len = page_table.shape[1] * PAGE_SIZE
    q_pos = past_len + jnp.arange(q.shape[1], dtype=jnp.int32)

    def one(args):
        qb, kn, vn, pt = args
        kf = jnp.concatenate([pool_k[layer, pt].reshape(past_len, ATT_HEADS, ATT_HEAD_DIM), kn], axis=0)
        vf = jnp.concatenate([pool_v[layer, pt].reshape(past_len, ATT_HEADS, ATT_HEAD_DIM), vn], axis=0)
        return moba_sequence(qb, kf, vf, q_pos, slopes)

    return lax.map(one, (q, k, v, page_table))


def token_mixer(h, conv_buf, h0, attend, w_in, conv_w, conv_b, dt_bias, a_log, d_skip, ssd_norm,
                w_proj_a, w_proj_b, w_out):
    b, L, _ = h.shape
    points = np.cumsum(IN_SIZES[:-1]).tolist()
    z, xbc, dt_raw, q, k, v, ga, gb = jnp.split(h @ w_in, points, axis=-1)
    y_a, new_conv, new_h = ssd_branch(z, xbc, dt_raw, conv_buf, h0, conv_w, conv_b, dt_bias, a_log,
                                      d_skip, ssd_norm)
    q = q.reshape(b, L, ATT_HEADS, ATT_HEAD_DIM)
    k = k.reshape(b, L, ATT_HEADS, ATT_HEAD_DIM)
    v = v.reshape(b, L, ATT_HEADS, ATT_HEAD_DIM)
    y_b = attend(q, k, v).reshape(b, L, ATT_WIDTH)
    merged = jax.nn.sigmoid(ga) * (y_a @ w_proj_a) + jax.nn.sigmoid(gb) * (y_b @ w_proj_b)
    return merged @ w_out, k, v, new_conv, new_h


def setup_inputs(seed: int = 0) -> dict:
    key = jax.random.key(seed)
    ks = jax.random.split(key, 32)
    f32 = jnp.float32
    n_pages = PAST_LEN // PAGE_SIZE
    n_used = DEC_BATCH * n_pages
    n_pool = n_used + n_used // 4
    nrm = lambda k, shape, s: jax.random.normal(k, shape, f32) * s
    x_prompt = nrm(ks[0], (BATCH, SEQ, D_MODEL), 1.0)
    x_sample = nrm(ks[1], (DEC_BATCH, DEC_SEQ, D_MODEL), 1.0)
    cache_k = nrm(ks[2], (DEPTH, n_pool, PAGE_SIZE, ATT_HEADS, ATT_HEAD_DIM), 1.0)
    cache_v = nrm(ks[3], (DEPTH, n_pool, PAGE_SIZE, ATT_HEADS, ATT_HEAD_DIM), 1.0)
    state_ssm = nrm(ks[4], (DEPTH, DEC_BATCH, SSD_HEADS, SSD_HEADDIM, D_STATE), 0.1)
    state_conv = nrm(ks[5], (DEPTH, DEC_BATCH, CONV_W - 1, CONV_DIM), 1.0)
    page_table = jax.random.permutation(ks[6], n_pool)[:n_used].reshape(DEC_BATCH, n_pages).astype(jnp.int32)
    gain = lambda k, shape: 1.0 + 0.1 * jax.random.normal(k, shape, f32)
    ln_ffn1 = gain(ks[7], (DEPTH, D_MODEL))
    w_ffn1_in = nrm(ks[8], (DEPTH, D_MODEL, 2 * D_FF), D_MODEL ** -0.5)
    w_ffn1_out = nrm(ks[9], (DEPTH, D_FF, D_MODEL), D_FF ** -0.5)
    ln_mix = gain(ks[10], (DEPTH, D_MODEL))
    w_in = nrm(ks[11], (DEPTH, D_MODEL, IN_COLS), D_MODEL ** -0.5)
    conv_w = nrm(ks[12], (DEPTH, CONV_W, CONV_DIM), CONV_W ** -0.5)
    conv_b = nrm(ks[13], (DEPTH, CONV_DIM), 0.02)
    dt0 = jnp.exp(jax.random.uniform(ks[14], (DEPTH, SSD_HEADS), f32, jnp.log(1e-3), jnp.log(1e-1)))
    dt_bias = dt0 + jnp.log(-jnp.expm1(-dt0))
    a_log = jnp.log(jax.random.uniform(ks[15], (DEPTH, SSD_HEADS), f32, 1.0, 16.0))
    d_skip = gain(ks[16], (DEPTH, SSD_HEADS))
    ssd_norm = gain(ks[17], (DEPTH, D_INNER))
    w_proj_a = nrm(ks[18], (DEPTH, D_INNER, D_MODEL), D_INNER ** -0.5)
    w_proj_b = nrm(ks[19], (DEPTH, ATT_WIDTH, D_MODEL), ATT_WIDTH ** -0.5)
    w_out = nrm(ks[20], (DEPTH, D_MODEL, D_MODEL), D_MODEL ** -0.5)
    ln_ffn2 = gain(ks[21], (DEPTH, D_MODEL))
    w_ffn2_in = nrm(ks[22], (DEPTH, D_MODEL, 2 * D_FF), D_MODEL ** -0.5)
    w_ffn2_out = nrm(ks[23], (DEPTH, D_FF, D_MODEL), D_FF ** -0.5)
    ln_final = gain(ks[24], (D_MODEL,))
    return {"x_prompt": x_prompt, "x_sample": x_sample, "cache_k": cache_k, "cache_v": cache_v,
            "state_ssm": state_ssm, "state_conv": state_conv, "page_table": page_table,
            "ln_ffn1": ln_ffn1, "w_ffn1_in": w_ffn1_in, "w_ffn1_out": w_ffn1_out,
            "ln_mix": ln_mix, "w_in": w_in, "conv_w": conv_w, "conv_b": conv_b,
            "dt_bias": dt_bias, "a_log": a_log, "d_skip": d_skip, "ssd_norm": ssd_norm,
            "w_proj_a": w_proj_a, "w_proj_b": w_proj_b, "w_out": w_out,
            "ln_ffn2": ln_ffn2, "w_ffn2_in": w_ffn2_in, "w_ffn2_out": w_ffn2_out,
            "ln_final": ln_final}


def reference(x_prompt, x_sample, cache_k, cache_v, state_ssm, state_conv, page_table,
              ln_ffn1, w_ffn1_in, w_ffn1_out, ln_mix, w_in, conv_w, conv_b, dt_bias, a_log, d_skip,
              ssd_norm, w_proj_a, w_proj_b, w_out, ln_ffn2, w_ffn2_in, w_ffn2_out, ln_final):
    slopes = jnp.exp2(-8.0 * jnp.arange(1, ATT_HEADS + 1, dtype=jnp.float32) / ATT_HEADS)
    bp = x_prompt.shape[0]
    xp, xs = x_prompt, x_sample
    kp_l, vp_l, hp_l, cp_l, ks_l, vs_l, hs_l, cs_l = [], [], [], [], [], [], [], []
    attend_p = functools.partial(moba_prompt, slopes=slopes)
    for l in range(DEPTH):
        mix_w = (w_in[l], conv_w[l], conv_b[l], dt_bias[l], a_log[l], d_skip[l], ssd_norm[l],
                 w_proj_a[l], w_proj_b[l], w_out[l])
        xp = half_step_ffn(xp, ln_ffn1[l], w_ffn1_in[l], w_ffn1_out[l])
        xs = half_step_ffn(xs, ln_ffn1[l], w_ffn1_in[l], w_ffn1_out[l])
        conv0 = jnp.zeros((bp, CONV_W - 1, CONV_DIM), xp.dtype)
        h00 = jnp.zeros((bp, SSD_HEADS, SSD_HEADDIM, D_STATE), xp.dtype)
        mp, kp, vp, cp, hp = token_mixer(rmsnorm(xp, ln_mix[l]), conv0, h00, attend_p, *mix_w)
        attend_s = functools.partial(moba_sample, pool_k=cache_k, pool_v=cache_v, page_table=page_table,
                                     layer=l, slopes=slopes)
        ms, ks, vs, cs, hs = token_mixer(rmsnorm(xs, ln_mix[l]), state_conv[l], state_ssm[l], attend_s, *mix_w)
        xp = half_step_ffn(xp + mp, ln_ffn2[l], w_ffn2_in[l], w_ffn2_out[l])
        xs = half_step_ffn(xs + ms, ln_ffn2[l], w_ffn2_in[l], w_ffn2_out[l])
        kp_l.append(kp); vp_l.append(vp); hp_l.append(hp); cp_l.append(cp)
        ks_l.append(ks); vs_l.append(vs); hs_l.append(hs); cs_l.append(cs)
    y_prompt = rmsnorm(xp, ln_final)
    y_sample = rmsnorm(xs, ln_final)
    k_prompt = jnp.stack(kp_l)
    v_prompt = jnp.stack(vp_l)
    ssm_prompt = jnp.stack(hp_l)
    conv_prompt = jnp.stack(cp_l)
    k_sample = jnp.stack(ks_l)
    v_sample = jnp.stack(vs_l)
    ssm_sample = jnp.stack(hs_l)
    conv_sample = jnp.stack(cs_l)
    return (y_prompt, y_sample, k_prompt, v_prompt, ssm_prompt, conv_prompt,
            k_sample, v_sample, ssm_sample, conv_sample)
```

```python
import functools

import jax
import jax.numpy as jnp
from jax import lax
from jax.experimental import pallas as pl
from jax.experimental.pallas import tpu as pltpu

F32 = jnp.float32
BF16 = jnp.bfloat16

D_MODEL = 1024
D_INNER = 2048
SSD_HEADDIM = 64
SSD_HEADS = 32
SSD_GROUPS = 4
GROUP_W = D_INNER // SSD_GROUPS
D_STATE = 128
CONV_W = 4
CONV_DIM = 3072
SSD_CHUNK = 128
ATT_HEADS = 8
ATT_HEAD_DIM = 128
ATT_WIDTH = 1024
MOBA_BLOCK = 256
MOBA_TOPK = 3
PAGE_SIZE = 128
D_FF = 2816
FFN_SCALE = 0.5
EPS = 1e-6

LANES = 128
SUBLANES = 8
VMEM_LIMIT = 48 * 1024 * 1024

P_COLS = 10240
COL_Z = 0
COL_GA = 2048
COL_XBC = 3072
COL_Q = 6144
COL_K = 7168
COL_V = 8192
COL_GB = 9216

_NT = (((1,), (1,)), ((), ()))
_TN = (((0,), (0,)), ((), ()))


def _params(sem):
    return pltpu.CompilerParams(dimension_semantics=sem, vmem_limit_bytes=VMEM_LIMIT)


def _rms(x, g):
    return x * lax.rsqrt(jnp.mean(x * x, axis=-1, keepdims=True) + EPS) * g


def _silu(x):
    return x * jax.nn.sigmoid(x)


def _split3(x):
    x1 = x.astype(BF16)
    r = x - x1.astype(F32)
    x2 = r.astype(BF16)
    r = r - x2.astype(F32)
    return x1, x2, r.astype(BF16)


def _sel_left(a01, x):
    return sum(jnp.dot(a01, p, preferred_element_type=F32) for p in _split3(x))


def _sel_right(x, e01):
    return sum(jnp.dot(p, e01, preferred_element_type=F32) for p in _split3(x))


def _ffn_kernel(x_ref, g_ref, wa_ref, wb_ref, wo_ref, o_ref, h_scr, acc_scr, *, nf):
    f = pl.program_id(1)

    @pl.when(f == 0)
    def _():
        h_scr[...] = _rms(x_ref[...], g_ref[...]).astype(BF16)
        acc_scr[...] = jnp.zeros_like(acc_scr)

    h = h_scr[...]
    a = jnp.dot(h, wa_ref[...], preferred_element_type=F32)
    b = jnp.dot(h, wb_ref[...], preferred_element_type=F32)
    acc_scr[...] += jnp.dot((_silu(a) * b).astype(BF16), wo_ref[...], preferred_element_type=F32)

    @pl.when(f == nf - 1)
    def _():
        o_ref[...] = x_ref[...] + FFN_SCALE * acc_scr[...]


def _ffn(x, g, w_in, w_out):
    t = x.shape[0]
    tm = min(512, t)
    nf = 2
    tf = D_FF // nf
    return pl.pallas_call(
        functools.partial(_ffn_kernel, nf=nf),
        grid=(t // tm, nf),
        in_specs=[
            pl.BlockSpec((tm, D_MODEL), lambda i, f: (i, 0)),
            pl.BlockSpec((1, D_MODEL), lambda i, f: (0, 0)),
            pl.BlockSpec((D_MODEL, tf), lambda i, f: (0, f)),
            pl.BlockSpec((D_MODEL, tf), lambda i, f: (0, f + nf)),
            pl.BlockSpec((tf, D_MODEL), lambda i, f: (f, 0)),
        ],
        out_specs=pl.BlockSpec((tm, D_MODEL), lambda i, f: (i, 0)),
        out_shape=jax.ShapeDtypeStruct((t, D_MODEL), F32),
        scratch_shapes=[pltpu.VMEM((tm, D_MODEL), BF16), pltpu.VMEM((tm, D_MODEL), F32)],
        compiler_params=_params(("parallel", "arbitrary")),
        name="ffn",
    )(x, g, w_in, w_in, w_out)


def _inproj_kernel(x_ref, g_ref, w_ref, wdt_ref, p_ref, dt_ref, h_scr):
    n = pl.program_id(1)

    @pl.when(n == 0)
    def _():
        h = _rms(x_ref[...], g_ref[...])
        h_scr[...] = h.astype(BF16)
        dt_ref[...] = jnp.dot(h, wdt_ref[...], preferred_element_type=F32,
                              precision=lax.Precision.HIGHEST)

    p_ref[...] = jnp.dot(h_scr[...], w_ref[...], preferred_element_type=F32)


def _inproj(x, g, w_main, w_dt):
    t = x.shape[0]
    tm = min(512, t)
    tn = 1024
    return pl.pallas_call(
        _inproj_kernel,
        grid=(t // tm, P_COLS // tn),
        in_specs=[
            pl.BlockSpec((tm, D_MODEL), lambda i, n: (i, 0)),
            pl.BlockSpec((1, D_MODEL), lambda i, n: (0, 0)),
            pl.BlockSpec((D_MODEL, tn), lambda i, n: (0, n)),
            pl.BlockSpec((D_MODEL, LANES), lambda i, n: (0, 0)),
        ],
        out_specs=[
            pl.BlockSpec((tm, tn), lambda i, n: (i, n)),
            pl.BlockSpec((tm, LANES), lambda i, n: (i, 0)),
        ],
        out_shape=[jax.ShapeDtypeStruct((t, P_COLS), F32), jax.ShapeDtypeStruct((t, LANES), F32)],
        scratch_shapes=[pltpu.VMEM((tm, D_MODEL), BF16)],
        compiler_params=_params(("parallel", "arbitrary")),
        name="inproj",
    )(x, g, w_main, w_dt)


_HIST = SUBLANES - (CONV_W - 1)


def _ssd_kernel(*refs, q_real, q_pad, nc, has_init):
    if has_init:
        (xbc_ref, z_ref, dt_ref, h0_ref, cbuf_ref, cw_ref, cb_ref, dtb_ref, alog_ref, dsk_ref,
         ng_ref, e_ref, et_ref, y_ref, hout_ref, ext_scr, h_scr, dt_scr, z_scr) = refs
    else:
        (xbc_ref, z_ref, dt_ref, cw_ref, cb_ref, dtb_ref, alog_ref, dsk_ref,
         ng_ref, e_ref, et_ref, y_ref, hout_ref, ext_scr, h_scr, dt_scr, z_scr) = refs
    c = pl.program_id(1)

    @pl.when(c == 0)
    def _():
        ext_scr[...] = jnp.zeros_like(ext_scr)
        dt_scr[...] = jnp.zeros_like(dt_scr)
        z_scr[...] = jnp.zeros_like(z_scr)
        if has_init:
            h_scr[...] = h0_ref[0, 0]
            ext_scr[_HIST:SUBLANES, :] = cbuf_ref[0, 0]
        else:
            h_scr[...] = jnp.zeros_like(h_scr)

    ext_scr[SUBLANES:SUBLANES + q_real, :] = xbc_ref[0]
    acc = cb_ref[...] + cw_ref[0:1, :] * ext_scr[_HIST:_HIST + q_pad, :]
    for w in range(1, CONV_W):
        acc = acc + cw_ref[w:w + 1, :] * ext_scr[_HIST + w:_HIST + w + q_pad, :]
    if nc > 1:
        ext_scr[_HIST:SUBLANES, :] = ext_scr[SUBLANES + q_real - (CONV_W - 1):SUBLANES + q_real, :]
    u = _silu(acc)
    xs = u[:, :D_INNER]
    b_bf = u[:, D_INNER:D_INNER + SSD_GROUPS * D_STATE].astype(BF16)
    c_bf = u[:, D_INNER + SSD_GROUPS * D_STATE:].astype(BF16)

    dt_scr[0:q_real, :] = dt_ref[0]
    z_scr[0:q_real, :] = z_ref[0]
    rows = lax.broadcasted_iota(jnp.int32, (q_pad, LANES), 0)
    dtr = dt_scr[...] + dtb_ref[...]
    softplus = jnp.maximum(dtr, 0.0) + jnp.log1p(jnp.exp(-jnp.abs(dtr)))
    dt = jnp.where(rows < q_real, softplus, 0.0)
    da = dt * (-jnp.exp(alog_ref[...]))

    ri = lax.broadcasted_iota(jnp.int32, (q_pad, q_pad), 0)
    ci = lax.broadcasted_iota(jnp.int32, (q_pad, q_pad), 1)
    causal = ri >= ci
    a_cum = _sel_left(causal.astype(BF16), da)
    a_last = a_cum[q_pad - 1:q_pad, :]
    if q_pad < LANES:
        a_sq = jnp.concatenate([a_cum, jnp.zeros((LANES - q_pad, LANES), F32)], axis=0)
    else:
        a_sq = a_cum
    a_cum_t = a_sq.T

    e01 = e_ref[...]
    dt_x = _sel_right(dt, e01)
    eac_x = _sel_right(jnp.exp(a_cum), e01)
    dte_x = _sel_right(jnp.exp(a_last - a_cum), e01)
    xw = xs * dt_x
    xw_bf = xw.astype(BF16)
    xwd_bf = (xw * dte_x).astype(BF16)

    dec_rows = jnp.broadcast_to(jnp.exp(a_cum_t[:, q_pad - 1:q_pad]), (LANES, LANES))
    dec = _sel_left(et_ref[...], dec_rows)

    lane = lax.broadcasted_iota(jnp.int32, (q_pad, LANES), 1)
    ys = []
    for g in range(SSD_GROUPS):
        cg = c_bf[:, g * D_STATE:(g + 1) * D_STATE]
        bg = b_bf[:, g * D_STATE:(g + 1) * D_STATE]
        cb = lax.dot_general(cg, bg, _NT, preferred_element_type=F32)
        hg = h_scr[g * GROUP_W:(g + 1) * GROUP_W, :]
        y_off = lax.dot_general(cg, hg.astype(BF16), _NT, preferred_element_type=F32)
        parts = []
        for k in range(4):
            slab = g * 4 + k
            xwp = xw_bf[:, slab * LANES:(slab + 1) * LANES]
            pair = []
            for h in (2 * slab, 2 * slab + 1):
                seg = a_cum[:, h:h + 1] - a_cum_t[h:h + 1, :q_pad]
                decay_in = jnp.exp(jnp.where(causal, seg, -jnp.inf))
                pair.append(jnp.dot((cb * decay_in).astype(BF16), xwp, preferred_element_type=F32))
            parts.append(jnp.where(lane < SSD_HEADDIM, pair[0], pair[1]))
        y_diag = jnp.concatenate(parts, axis=1)
        ys.append(y_diag + y_off * eac_x[:, g * GROUP_W:(g + 1) * GROUP_W])
        s_new = lax.dot_general(xwd_bf[:, g * GROUP_W:(g + 1) * GROUP_W], bg, _TN,
                                preferred_element_type=F32)
        h_scr[g * GROUP_W:(g + 1) * GROUP_W, :] = hg * dec[g * GROUP_W:(g + 1) * GROUP_W, :] + s_new

    y = jnp.concatenate(ys, axis=1) + dsk_ref[...] * xs
    gated = y * _silu(z_scr[...])
    outs = []
    for g in range(SSD_GROUPS):
        ug = gated[:, g * GROUP_W:(g + 1) * GROUP_W]
        outs.append(ug * lax.rsqrt(jnp.mean(ug * ug, axis=-1, keepdims=True) + EPS))
    out = jnp.concatenate(outs, axis=1) * ng_ref[...]
    y_ref[0] = out[0:q_real, :]

    @pl.when(c == nc - 1)
    def _():
        hout_ref[0] = h_scr[...]


def _ssd(p3, dt3, consts, state=None):
    b, L, _ = p3.shape
    q_real = min(SSD_CHUNK, L)
    q_pad = -(-q_real // SUBLANES) * SUBLANES
    nc = L // q_real
    has_init = state is not None
    cw, cbias, dtb, alog, dsk, ng, e01, et01 = consts

    def const(shape):
        return pl.BlockSpec(shape, lambda i, c: (0,) * len(shape))

    in_specs = [
        pl.BlockSpec((1, q_real, CONV_DIM), lambda i, c: (i, c, COL_XBC // CONV_DIM)),
        pl.BlockSpec((1, q_real, D_INNER), lambda i, c: (i, c, COL_Z // D_INNER)),
        pl.BlockSpec((1, q_real, LANES), lambda i, c: (i, c, 0)),
    ]
    args = [p3, p3, dt3]
    if has_init:
        h0, cbuf, layer = state
        in_specs += [
            pl.BlockSpec((1, 1, D_INNER, D_STATE), lambda i, c: (layer, i, 0, 0)),
            pl.BlockSpec((1, 1, CONV_W - 1, CONV_DIM), lambda i, c: (layer, i, 0, 0)),
        ]
        args += [h0, cbuf]
    in_specs += [const((CONV_W, CONV_DIM)), const((1, CONV_DIM)), const((1, LANES)), const((1, LANES)),
                 const((1, D_INNER)), const((1, D_INNER)), const((LANES, D_INNER)), const((D_INNER, LANES))]
    args += [cw, cbias, dtb, alog, dsk, ng, e01, et01]
    return pl.pallas_call(
        functools.partial(_ssd_kernel, q_real=q_real, q_pad=q_pad, nc=nc, has_init=has_init),
        grid=(b, nc),
        in_specs=in_specs,
        out_specs=[
            pl.BlockSpec((1, q_real, D_INNER), lambda i, c: (i, c, 0)),
            pl.BlockSpec((1, D_INNER, D_STATE), lambda i, c: (i, 0, 0)),
        ],
        out_shape=[jax.ShapeDtypeStruct((b, L, D_INNER), F32),
                   jax.ShapeDtypeStruct((b, D_INNER, D_STATE), F32)],
        scratch_shapes=[
            pltpu.VMEM((SUBLANES + q_pad, CONV_DIM), F32),
            pltpu.VMEM((D_INNER, D_STATE), F32),
            pltpu.VMEM((q_pad, LANES), F32),
            pltpu.VMEM((q_pad, D_INNER), F32),
        ],
        compiler_params=_params(("parallel", "arbitrary")),
        name="ssd_s" if has_init else "ssd_p",
    )(*args)


def _moba_p_kernel(slopes_ref, q_ref, k_ref, v_ref, o_ref, kmean_scr, k_scr, vt_scr, sel_scr, *, nb):
    h = pl.program_id(1)
    j = pl.program_id(2)
    blk = MOBA_BLOCK

    @pl.when(j == 0)
    def _():
        kmean_scr[...] = jnp.zeros_like(kmean_scr)
        for n in range(nb):
            kn = k_ref[0, n * blk:(n + 1) * blk, :]
            kmean_scr[n:n + 1, :] = jnp.sum(kn, axis=0, keepdims=True) * (1.0 / blk)
            k_scr[n] = kn.astype(BF16)
            vt_scr[n] = v_ref[0, n * blk:(n + 1) * blk, :].T.astype(BF16)

    q = q_ref[0]
    q_bf = q.astype(BF16)
    gate = lax.dot_general(kmean_scr[...], q, _NT, preferred_element_type=F32,
                           precision=lax.Precision.HIGHEST)[0:SUBLANES, :]
    rowid = lax.broadcasted_iota(jnp.int32, (SUBLANES, blk), 0)
    gate = jnp.where(rowid < j, gate, -jnp.inf)
    cnt = jnp.zeros((SUBLANES, blk), F32)
    for m in range(nb):
        gm = gate[m:m + 1, :]
        beats = (gm > gate) | ((gm == gate) & (m < rowid))
        cnt = cnt + jnp.where(beats, 1.0, 0.0)
    sel_scr[...] = jnp.where((cnt < MOBA_TOPK) & (rowid < j), 1.0, 0.0)

    slope = slopes_ref[h]
    scale = ATT_HEAD_DIM ** -0.5
    krow = lax.broadcasted_iota(jnp.int32, (blk, blk), 0)
    qcol = lax.broadcasted_iota(jnp.int32, (blk, blk), 1)
    rel = (qcol - krow).astype(F32)

    def scores(n, dist):
        s = lax.dot_general(k_scr[n], q_bf, _NT, preferred_element_type=F32) * scale
        return s - slope * (rel + dist)

    s = jnp.where(krow <= qcol, scores(j, 0.0), -jnp.inf)
    m0 = jnp.max(s, axis=0, keepdims=True)
    p = jnp.exp(s - m0)
    l0 = jnp.sum(p, axis=0, keepdims=True)
    acc0 = jnp.dot(vt_scr[j], p.astype(BF16), preferred_element_type=F32)

    def body(n, carry):
        m_i, l_i, acc = carry
        dist = ((j - n) * blk).astype(F32)
        s = jnp.where(sel_scr[pl.ds(n, 1), :] > 0.0, scores(n, dist), -jnp.inf)
        m_new = jnp.maximum(m_i, jnp.max(s, axis=0, keepdims=True))
        alpha = jnp.exp(m_i - m_new)
        p = jnp.exp(s - m_new)
        l_new = alpha * l_i + jnp.sum(p, axis=0, keepdims=True)
        acc_new = alpha * acc + jnp.dot(vt_scr[n], p.astype(BF16), preferred_element_type=F32)
        return m_new, l_new, acc_new

    _, l_f, acc_f = lax.fori_loop(0, j, body, (m0, l0, acc0))
    o_ref[0] = (acc_f / l_f).T


def _moba_p(p3, slopes):
    b, L, _ = p3.shape
    nb = L // MOBA_BLOCK
    cq, ck, cv = COL_Q // LANES, COL_K // LANES, COL_V // LANES
    return pl.pallas_call(
        functools.partial(_moba_p_kernel, nb=nb),
        grid=(b, ATT_HEADS, nb),
        in_specs=[
            pl.BlockSpec(memory_space=pltpu.SMEM),
            pl.BlockSpec((1, MOBA_BLOCK, ATT_HEAD_DIM), lambda i, h, j: (i, j, cq + h)),
            pl.BlockSpec((1, L, ATT_HEAD_DIM), lambda i, h, j: (i, 0, ck + h)),
            pl.BlockSpec((1, L, ATT_HEAD_DIM), lambda i, h, j: (i, 0, cv + h)),
        ],
        out_specs=pl.BlockSpec((1, MOBA_BLOCK, ATT_HEAD_DIM), lambda i, h, j: (i, j, h)),
        out_shape=jax.ShapeDtypeStruct((b, L, ATT_WIDTH), F32),
        scratch_shapes=[
            pltpu.VMEM((LANES, ATT_HEAD_DIM), F32),
            pltpu.VMEM((nb, MOBA_BLOCK, ATT_HEAD_DIM), BF16),
            pltpu.VMEM((nb, ATT_HEAD_DIM, MOBA_BLOCK), BF16),
            pltpu.VMEM((SUBLANES, MOBA_BLOCK), F32),
        ],
        compiler_params=_params(("parallel", "parallel", "arbitrary")),
        name="moba_p",
    )(slopes, p3, p3, p3)


def _moba_s_kernel(pt_ref, q_ref, kn_ref, vn_ref, k0_ref, k1_ref, v0_ref, v1_ref, o_ref,
                   qf_scr, acc_scr, m_scr, l_scr, g_scr, *, nb, lq, past_len):
    n = pl.program_id(1)
    rows = lq * ATT_HEADS
    rid = lax.broadcasted_iota(jnp.int32, (rows, 1), 0)
    head = rid % ATT_HEADS
    t_pos = (past_len + rid // ATT_HEADS).astype(F32)
    slope = jnp.exp2(-(head + 1).astype(F32) * (8.0 / ATT_HEADS))
    scale = ATT_HEAD_DIM ** -0.5

    @pl.when(n == 0)
    def _():
        q = q_ref[0]
        rep = jnp.concatenate(
            [jnp.broadcast_to(q[i:i + 1, :], (ATT_HEADS, ATT_WIDTH)) for i in range(lq)], axis=0)
        col_head = lax.broadcasted_iota(jnp.int32, (rows, ATT_WIDTH), 1) // ATT_HEAD_DIM
        qf_scr[...] = jnp.where(col_head == head, rep, 0.0)

    qf = qf_scr[...]
    q_bf = qf.astype(BF16)
    k0 = k0_ref[0, 0]
    k1 = k1_ref[0, 0]
    kmean = (jnp.sum(k0, axis=0, keepdims=True) + jnp.sum(k1, axis=0, keepdims=True)) * (1.0 / MOBA_BLOCK)
    gate = jnp.sum(qf * kmean, axis=-1, keepdims=True)

    def half_scores(kh, off):
        s = lax.dot_general(q_bf, kh.astype(BF16), _NT, preferred_element_type=F32) * scale
        kpos = (n * MOBA_BLOCK + off + lax.broadcasted_iota(jnp.int32, (rows, PAGE_SIZE), 1)).astype(F32)
        return s - slope * (t_pos - kpos)

    s0 = half_scores(k0, 0)
    s1 = half_scores(k1, PAGE_SIZE)
    m_n = jnp.maximum(jnp.max(s0, axis=-1, keepdims=True), jnp.max(s1, axis=-1, keepdims=True))
    p0 = jnp.exp(s0 - m_n)
    p1 = jnp.exp(s1 - m_n)
    l_n = jnp.sum(p0, axis=-1, keepdims=True) + jnp.sum(p1, axis=-1, keepdims=True)
    acc_scr[n] = (jnp.dot(p0.astype(BF16), v0_ref[0, 0].astype(BF16), preferred_element_type=F32)
                  + jnp.dot(p1.astype(BF16), v1_ref[0, 0].astype(BF16), preferred_element_type=F32))
    m_scr[n] = jnp.broadcast_to(m_n, (rows, LANES))
    l_scr[n] = jnp.broadcast_to(l_n, (rows, LANES))
    g_scr[n] = jnp.broadcast_to(gate, (rows, LANES))

    @pl.when(n == nb - 1)
    def _():
        qidx = rid // ATT_HEADS
        s_own = []
        for kk in range(lq):
            s = jnp.sum(qf * kn_ref[0, kk:kk + 1, :], axis=-1, keepdims=True) * scale
            s = s - slope * (t_pos - float(past_len + kk))
            s_own.append(jnp.where(kk <= qidx, s, -jnp.inf))
        gates = [g_scr[i][:, 0:1] for i in range(nb)]
        ms = [m_scr[i][:, 0:1] for i in range(nb)]
        sel = []
        for i in range(nb):
            cnt = jnp.zeros((rows, 1), F32)
            for m in range(nb):
                beats = (gates[m] > gates[i]) | ((gates[m] == gates[i]) & (m < i))
                cnt = cnt + jnp.where(beats, 1.0, 0.0)
            sel.append(cnt < MOBA_TOPK)
        m_tot = s_own[0]
        for kk in range(1, lq):
            m_tot = jnp.maximum(m_tot, s_own[kk])
        for i in range(nb):
            m_tot = jnp.maximum(m_tot, jnp.where(sel[i], ms[i], -jnp.inf))
        l_tot = jnp.zeros((rows, 1), F32)
        acc = jnp.zeros((rows, ATT_WIDTH), F32)
        for kk in range(lq):
            w = jnp.exp(s_own[kk] - m_tot)
            l_tot = l_tot + w
            acc = acc + w * vn_ref[0, kk:kk + 1, :]
        for i in range(nb):
            w = jnp.where(sel[i], jnp.exp(ms[i] - m_tot), 0.0)
            l_tot = l_tot + w * l_scr[i][:, 0:1]
            acc = acc + w * acc_scr[i]
        col_head = lax.broadcasted_iota(jnp.int32, (rows, ATT_WIDTH), 1) // ATT_HEAD_DIM
        res = jnp.where(col_head == head, acc / l_tot, 0.0)
        o_ref[0] = jnp.sum(res.reshape(lq, ATT_HEADS, ATT_WIDTH), axis=1)


def _moba_s(ps3, cache_k, cache_v, page_table, layer):
    b, lq, _ = ps3.shape
    n_pages = page_table.shape[1]
    past_len = n_pages * PAGE_SIZE
    nb = past_len // MOBA_BLOCK
    ppb = MOBA_BLOCK // PAGE_SIZE
    assert ppb == 2 and nb >= MOBA_TOPK and lq <= MOBA_BLOCK
    rows = lq * ATT_HEADS
    cq, ck, cv = COL_Q // ATT_WIDTH, COL_K // ATT_WIDTH, COL_V // ATT_WIDTH

    def page(off):
        return pl.BlockSpec((1, 1, PAGE_SIZE, ATT_WIDTH),
                            lambda i, n, pt: (layer, pt[i, ppb * n + off], 0, 0))

    grid_spec = pltpu.PrefetchScalarGridSpec(
        num_scalar_prefetch=1,
        grid=(b, nb),
        in_specs=[
            pl.BlockSpec((1, lq, ATT_WIDTH), lambda i, n, pt: (i, 0, cq)),
            pl.BlockSpec((1, lq, ATT_WIDTH), lambda i, n, pt: (i, 0, ck)),
            pl.BlockSpec((1, lq, ATT_WIDTH), lambda i, n, pt: (i, 0, cv)),
            page(0), page(1), page(0), page(1),
        ],
        out_specs=pl.BlockSpec((1, lq, ATT_WIDTH), lambda i, n, pt: (i, 0, 0)),
        scratch_shapes=[
            pltpu.VMEM((rows, ATT_WIDTH), F32),
            pltpu.VMEM((nb, rows, ATT_WIDTH), F32),
            pltpu.VMEM((nb, rows, LANES), F32),
            pltpu.VMEM((nb, rows, LANES), F32),
            pltpu.VMEM((nb, rows, LANES), F32),
        ],
    )
    return pl.pallas_call(
        functools.partial(_moba_s_kernel, nb=nb, lq=lq, past_len=past_len),
        grid_spec=grid_spec,
        out_shape=jax.ShapeDtypeStruct((b, lq, ATT_WIDTH), F32),
        compiler_params=_params(("parallel", "arbitrary")),
        name="moba_s",
    )(page_table, ps3, ps3, ps3, cache_k, cache_k, cache_v, cache_v)


def _merge_kernel(x_ref, ya_ref, yb_ref, ga_ref, gb_ref, wa_ref, wb_ref, wo_ref, o_ref):
    a = jnp.dot(ya_ref[...].astype(BF16), wa_ref[...], preferred_element_type=F32)
    b = jnp.dot(yb_ref[...].astype(BF16), wb_ref[...], preferred_element_type=F32)
    m = jax.nn.sigmoid(ga_ref[...]) * a + jax.nn.sigmoid(gb_ref[...]) * b
    o_ref[...] = x_ref[...] + jnp.dot(m.astype(BF16), wo_ref[...], preferred_element_type=F32)


def _merge(x, ya, yb, p, wpa, wpb, wo):
    t = x.shape[0]
    tm = min(512, t)
    return pl.pallas_call(
        _merge_kernel,
        grid=(t // tm,),
        in_specs=[
            pl.BlockSpec((tm, D_MODEL), lambda i: (i, 0)),
            pl.BlockSpec((tm, D_INNER), lambda i: (i, 0)),
            pl.BlockSpec((tm, ATT_WIDTH), lambda i: (i, 0)),
            pl.BlockSpec((tm, D_MODEL), lambda i: (i, COL_GA // D_MODEL)),
            pl.BlockSpec((tm, D_MODEL), lambda i: (i, COL_GB // D_MODEL)),
            pl.BlockSpec((D_INNER, D_MODEL), lambda i: (0, 0)),
            pl.BlockSpec((ATT_WIDTH, D_MODEL), lambda i: (0, 0)),
            pl.BlockSpec((D_MODEL, D_MODEL), lambda i: (0, 0)),
        ],
        out_specs=pl.BlockSpec((tm, D_MODEL), lambda i: (i, 0)),
        out_shape=jax.ShapeDtypeStruct((t, D_MODEL), F32),
        compiler_params=_params(("parallel",)),
        name="merge",
    )(x, ya, yb, p, p, wpa, wpb, wo)


def _norm_kernel(x_ref, g_ref, o_ref):
    o_ref[...] = _rms(x_ref[...], g_ref[...])


def _final_norm(x, g):
    t = x.shape[0]
    tm = min(512, t)
    return pl.pallas_call(
        _norm_kernel,
        grid=(t // tm,),
        in_specs=[pl.BlockSpec((tm, D_MODEL), lambda i: (i, 0)), pl.BlockSpec((1, D_MODEL), lambda i: (0, 0))],
        out_specs=pl.BlockSpec((tm, D_MODEL), lambda i: (i, 0)),
        out_shape=jax.ShapeDtypeStruct((t, D_MODEL), F32),
        compiler_params=_params(("parallel",)),
        name="final_norm",
    )(x, g)


def _pad_lanes(v):
    return jnp.pad(v.astype(F32), (0, LANES - v.shape[0])).reshape(1, LANES)


def kernel(x_prompt, x_sample, cache_k, cache_v, state_ssm, state_conv, page_table, ln_ffn1, w_ffn1_in,
           w_ffn1_out, ln_mix, w_in, conv_w, conv_b, dt_bias, a_log, d_skip, ssd_norm, w_proj_a, w_proj_b,
           w_out, ln_ffn2, w_ffn2_in, w_ffn2_out, ln_final):
    depth = w_in.shape[0]
    bp, seq, _ = x_prompt.shape
    bs, lq, _ = x_sample.shape
    slopes = jnp.exp2(-8.0 * jnp.arange(1, ATT_HEADS + 1, dtype=F32) / ATT_HEADS)
    head_of = jnp.arange(D_INNER, dtype=jnp.int32) // SSD_HEADDIM
    e01 = (jnp.arange(LANES, dtype=jnp.int32)[:, None] == head_of[None, :]).astype(BF16)
    et01 = e01.T
    pool = cache_k.shape[1]
    ck = cache_k.reshape(depth, pool, PAGE_SIZE, ATT_WIDTH)
    cv = cache_v.reshape(depth, pool, PAGE_SIZE, ATT_WIDTH)
    h0_all = state_ssm.reshape(depth, bs, D_INNER, D_STATE)

    o_z, o_xbc, o_dt = 0, D_INNER, D_INNER + CONV_DIM
    o_q = o_dt + SSD_HEADS
    o_k, o_v, o_ga, o_gb = o_q + ATT_WIDTH, o_q + 2 * ATT_WIDTH, o_q + 3 * ATT_WIDTH, o_q + 4 * ATT_WIDTH

    xp = x_prompt.reshape(bp * seq, D_MODEL)
    xs = x_sample.reshape(bs * lq, D_MODEL)
    outs = {k: [] for k in ("kp", "vp", "hp", "cp", "ks", "vs", "hs", "cs")}
    row = lambda v: v.astype(F32).reshape(1, -1)
    for l in range(depth):
        wl = w_in[l]
        w_main = jnp.concatenate(
            [wl[:, o_z:o_xbc], wl[:, o_ga:o_gb], wl[:, o_xbc:o_dt], wl[:, o_q:o_ga], wl[:, o_gb:]],
            axis=1).astype(BF16)
        w_dt = jnp.pad(wl[:, o_dt:o_q], ((0, 0), (0, LANES - SSD_HEADS)))
        w1i, w1o = w_ffn1_in[l].astype(BF16), w_ffn1_out[l].astype(BF16)
        w2i, w2o = w_ffn2_in[l].astype(BF16), w_ffn2_out[l].astype(BF16)
        wpa, wpb, wo = w_proj_a[l].astype(BF16), w_proj_b[l].astype(BF16), w_out[l].astype(BF16)
        consts = (conv_w[l], row(conv_b[l]), _pad_lanes(dt_bias[l]), _pad_lanes(a_log[l]),
                  row(jnp.repeat(d_skip[l], SSD_HEADDIM)), row(ssd_norm[l]), e01, et01)
        g1, gm, g2 = row(ln_ffn1[l]), row(ln_mix[l]), row(ln_ffn2[l])

        xp = _ffn(xp, g1, w1i, w1o)
        xs = _ffn(xs, g1, w1i, w1o)
        pp, dtp = _inproj(xp, gm, w_main, w_dt)
        ps, dts = _inproj(xs, gm, w_main, w_dt)
        pp3 = pp.reshape(bp, seq, P_COLS)
        ps3 = ps.reshape(bs, lq, P_COLS)

        ya_p, h_p = _ssd(pp3, dtp.reshape(bp, seq, LANES), consts)
        ya_s, h_s = _ssd(ps3, dts.reshape(bs, lq, LANES), consts, state=(h0_all, state_conv, l))
        yb_p = _moba_p(pp3, slopes)
        yb_s = _moba_s(ps3, ck, cv, page_table, l)

        xp = _merge(xp, ya_p.reshape(bp * seq, D_INNER), yb_p.reshape(bp * seq, ATT_WIDTH), pp, wpa, wpb, wo)
        xs = _merge(xs, ya_s.reshape(bs * lq, D_INNER), yb_s.reshape(bs * lq, ATT_WIDTH), ps, wpa, wpb, wo)
        xp = _ffn(xp, g2, w2i, w2o)
        xs = _ffn(xs, g2, w2i, w2o)

        outs["kp"].append(pp3[:, :, COL_K:COL_K + ATT_WIDTH].reshape(bp, seq, ATT_HEADS, ATT_HEAD_DIM))
        outs["vp"].append(pp3[:, :, COL_V:COL_V + ATT_WIDTH].reshape(bp, seq, ATT_HEADS, ATT_HEAD_DIM))
        outs["hp"].append(h_p.reshape(bp, SSD_HEADS, SSD_HEADDIM, D_STATE))
        outs["cp"].append(pp3[:, seq - (CONV_W - 1):, COL_XBC:COL_XBC + CONV_DIM])
        outs["ks"].append(ps3[:, :, COL_K:COL_K + ATT_WIDTH].reshape(bs, lq, ATT_HEADS, ATT_HEAD_DIM))
        outs["vs"].append(ps3[:, :, COL_V:COL_V + ATT_WIDTH].reshape(bs, lq, ATT_HEADS, ATT_HEAD_DIM))
        outs["hs"].append(h_s.reshape(bs, SSD_HEADS, SSD_HEADDIM, D_STATE))
        if lq >= CONV_W - 1:
            cs = ps3[:, lq - (CONV_W - 1):, COL_XBC:COL_XBC + CONV_DIM]
        else:
            cs = jnp.concatenate([state_conv[l], ps3[:, :, COL_XBC:COL_XBC + CONV_DIM]], axis=1)[:, -(CONV_W - 1):]
        outs["cs"].append(cs)

    gf = row(ln_final)
    y_prompt = _final_norm(xp, gf).reshape(bp, seq, D_MODEL)
    y_sample = _final_norm(xs, gf).reshape(bs, lq, D_MODEL)
    st = lambda k: jnp.stack(outs[k])
    return (y_prompt, y_sample, st("kp"), st("vp"), st("hp"), st("cp"), st("ks"), st("vs"), st("hs"), st("cs"))
```

```python
import functools

import jax
import jax.numpy as jnp
from jax import lax
from jax.experimental import pallas as pl
from jax.experimental.pallas import tpu as pltpu

F32 = jnp.float32
BF16 = jnp.bfloat16

D_MODEL = 1024
D_INNER = 2048
SSD_HEADDIM = 64
SSD_HEADS = 32
SSD_GROUPS = 4
GROUP_W = D_INNER // SSD_GROUPS
D_STATE = 128
CONV_W = 4
CONV_DIM = 3072
SSD_CHUNK = 128
ATT_HEADS = 8
ATT_HEAD_DIM = 128
ATT_WIDTH = 1024
MOBA_BLOCK = 256
MOBA_TOPK = 3
PAGE_SIZE = 128
D_FF = 2816
FFN_SCALE = 0.5
EPS = 1e-6

LANES = 128
SUBLANES = 8
VMEM_LIMIT = 48 * 1024 * 1024

P_COLS = 10240
COL_Z = 0
COL_GA = 2048
COL_XBC = 3072
COL_Q = 6144
COL_K = 7168
COL_V = 8192
COL_GB = 9216

_NT = (((1,), (1,)), ((), ()))
_TN = (((0,), (0,)), ((), ()))


def _params(sem):
    return pltpu.CompilerParams(dimension_semantics=sem, vmem_limit_bytes=VMEM_LIMIT)


def _rms(x, g):
    return x * lax.rsqrt(jnp.mean(x * x, axis=-1, keepdims=True) + EPS) * g


def _silu(x):
    return x * jax.nn.sigmoid(x)


def _split3(x):
    x1 = x.astype(BF16)
    r = x - x1.astype(F32)
    x2 = r.astype(BF16)
    r = r - x2.astype(F32)
    return x1, x2, r.astype(BF16)


def _sel_left(a01, x, pieces=3):
    return sum(jnp.dot(a01, p, preferred_element_type=F32) for p in _split3(x)[:pieces])


def _sel_right(x, e01, pieces=3):
    return sum(jnp.dot(p, e01, preferred_element_type=F32) for p in _split3(x)[:pieces])


def _ffn_kernel(x_ref, g_ref, wa_ref, wb_ref, wo_ref, o_ref, h_scr, acc_scr, *, nf):
    f = pl.program_id(1)

    @pl.when(f == 0)
    def _():
        h_scr[...] = _rms(x_ref[...], g_ref[...]).astype(BF16)
        acc_scr[...] = jnp.zeros_like(acc_scr)

    h = h_scr[...]
    a = jnp.dot(h, wa_ref[...], preferred_element_type=F32)
    b = jnp.dot(h, wb_ref[...], preferred_element_type=F32)
    acc_scr[...] += jnp.dot((_silu(a) * b).astype(BF16), wo_ref[...], preferred_element_type=F32)

    @pl.when(f == nf - 1)
    def _():
        o_ref[...] = x_ref[...] + FFN_SCALE * acc_scr[...]


def _ffn(x, g, w_in, w_out):
    t = x.shape[0]
    tm = min(512, t)
    nf = 2
    tf = D_FF // nf
    return pl.pallas_call(
        functools.partial(_ffn_kernel, nf=nf),
        grid=(t // tm, nf),
        in_specs=[
            pl.BlockSpec((tm, D_MODEL), lambda i, f: (i, 0)),
            pl.BlockSpec((1, D_MODEL), lambda i, f: (0, 0)),
            pl.BlockSpec((D_MODEL, tf), lambda i, f: (0, f)),
            pl.BlockSpec((D_MODEL, tf), lambda i, f: (0, f + nf)),
            pl.BlockSpec((tf, D_MODEL), lambda i, f: (f, 0)),
        ],
        out_specs=pl.BlockSpec((tm, D_MODEL), lambda i, f: (i, 0)),
        out_shape=jax.ShapeDtypeStruct((t, D_MODEL), F32),
        scratch_shapes=[pltpu.VMEM((tm, D_MODEL), BF16), pltpu.VMEM((tm, D_MODEL), F32)],
        compiler_params=_params(("parallel", "arbitrary")),
        name="ffn",
    )(x, g, w_in, w_in, w_out)


def _inproj_kernel(x_ref, g_ref, w_ref, wdth_ref, wdtl_ref, p_ref, dt_ref, h_scr):
    n = pl.program_id(1)

    @pl.when(n == 0)
    def _():
        h = _rms(x_ref[...], g_ref[...])
        h_hi = h.astype(BF16)
        h_lo = (h - h_hi.astype(F32)).astype(BF16)
        h_scr[...] = h_hi
        dt_ref[...] = (jnp.dot(h_hi, wdth_ref[...], preferred_element_type=F32)
                       + jnp.dot(h_hi, wdtl_ref[...], preferred_element_type=F32)
                       + jnp.dot(h_lo, wdth_ref[...], preferred_element_type=F32))

    p_ref[...] = jnp.dot(h_scr[...], w_ref[...], preferred_element_type=F32)


def _inproj(x, g, w_main, w_dt_hi, w_dt_lo):
    t = x.shape[0]
    tm = 1024 if t % 1024 == 0 else min(512, t)
    tn = 1024
    return pl.pallas_call(
        _inproj_kernel,
        grid=(t // tm, P_COLS // tn),
        in_specs=[
            pl.BlockSpec((tm, D_MODEL), lambda i, n: (i, 0)),
            pl.BlockSpec((1, D_MODEL), lambda i, n: (0, 0)),
            pl.BlockSpec((D_MODEL, tn), lambda i, n: (0, n)),
            pl.BlockSpec((D_MODEL, LANES), lambda i, n: (0, 0)),
            pl.BlockSpec((D_MODEL, LANES), lambda i, n: (0, 0)),
        ],
        out_specs=[
            pl.BlockSpec((tm, tn), lambda i, n: (i, n)),
            pl.BlockSpec((tm, LANES), lambda i, n: (i, 0)),
        ],
        out_shape=[jax.ShapeDtypeStruct((t, P_COLS), F32), jax.ShapeDtypeStruct((t, LANES), F32)],
        scratch_shapes=[pltpu.VMEM((tm, D_MODEL), BF16)],
        compiler_params=_params(("parallel", "arbitrary")),
        name="inproj",
    )(x, g, w_main, w_dt_hi, w_dt_lo)


_HIST = SUBLANES - (CONV_W - 1)


def _ssd_kernel(*refs, q_real, q_pad, nc, has_init):
    if has_init:
        (xbc_ref, z_ref, dt_ref, h0_ref, cbuf_ref, cw_ref, cb_ref, dtb_ref, alog_ref, dsk_ref,
         ng_ref, e_ref, et_ref, y_ref, hout_ref, ext_scr, h_scr, dt_scr, z_scr) = refs
    else:
        (xbc_ref, z_ref, dt_ref, cw_ref, cb_ref, dtb_ref, alog_ref, dsk_ref,
         ng_ref, e_ref, et_ref, y_ref, hout_ref, ext_scr, h_scr, dt_scr, z_scr) = refs
    c = pl.program_id(1)

    @pl.when(c == 0)
    def _():
        ext_scr[...] = jnp.zeros_like(ext_scr)
        dt_scr[...] = jnp.zeros_like(dt_scr)
        z_scr[...] = jnp.zeros_like(z_scr)
        if has_init:
            h_scr[...] = h0_ref[0, 0]
            ext_scr[_HIST:SUBLANES, :] = cbuf_ref[0, 0]
        else:
            h_scr[...] = jnp.zeros_like(h_scr)

    ext_scr[SUBLANES:SUBLANES + q_real, :] = xbc_ref[0]
    acc = cb_ref[...] + cw_ref[0:1, :] * ext_scr[_HIST:_HIST + q_pad, :]
    for w in range(1, CONV_W):
        acc = acc + cw_ref[w:w + 1, :] * ext_scr[_HIST + w:_HIST + w + q_pad, :]
    if nc > 1:
        ext_scr[_HIST:SUBLANES, :] = ext_scr[SUBLANES + q_real - (CONV_W - 1):SUBLANES + q_real, :]
    u = _silu(acc)
    xs = u[:, :D_INNER]
    b_bf = u[:, D_INNER:D_INNER + SSD_GROUPS * D_STATE].astype(BF16)
    c_bf = u[:, D_INNER + SSD_GROUPS * D_STATE:].astype(BF16)

    dt_scr[0:q_real, :] = dt_ref[0]
    z_scr[0:q_real, :] = z_ref[0]
    rows = lax.broadcasted_iota(jnp.int32, (q_pad, LANES), 0)
    dtr = dt_scr[...] + dtb_ref[...]
    softplus = jnp.maximum(dtr, 0.0) + jnp.log1p(jnp.exp(-jnp.abs(dtr)))
    dt = jnp.where(rows < q_real, softplus, 0.0)
    da = dt * (-jnp.exp(alog_ref[...]))

    ri = lax.broadcasted_iota(jnp.int32, (q_pad, q_pad), 0)
    ci = lax.broadcasted_iota(jnp.int32, (q_pad, q_pad), 1)
    causal = ri >= ci
    a_cum = _sel_left(causal.astype(BF16), da)
    a_last = a_cum[q_pad - 1:q_pad, :]
    if q_pad < LANES:
        a_sq = jnp.concatenate([a_cum, jnp.zeros((LANES - q_pad, LANES), F32)], axis=0)
    else:
        a_sq = a_cum
    a_cum_t = a_sq.T

    e01 = e_ref[...]
    dt_x = _sel_right(dt, e01, pieces=2)
    eac_x = _sel_right(jnp.exp(a_cum), e01, pieces=2)
    dte_x = _sel_right(jnp.exp(a_last - a_cum), e01, pieces=2)
    xw = xs * dt_x
    xw_bf = xw.astype(BF16)
    xwd_bf = (xw * dte_x).astype(BF16)

    dec_rows = jnp.broadcast_to(jnp.exp(a_cum_t[:, q_pad - 1:q_pad]), (LANES, LANES))
    dec = _sel_left(et_ref[...], dec_rows, pieces=2)

    lane = lax.broadcasted_iota(jnp.int32, (q_pad, LANES), 1)
    ys = []
    for g in range(SSD_GROUPS):
        cg = c_bf[:, g * D_STATE:(g + 1) * D_STATE]
        bg = b_bf[:, g * D_STATE:(g + 1) * D_STATE]
        cb = lax.dot_general(cg, bg, _NT, preferred_element_type=F32)
        hg = h_scr[g * GROUP_W:(g + 1) * GROUP_W, :]
        y_off = lax.dot_general(cg, hg.astype(BF16), _NT, preferred_element_type=F32)
        parts = []
        for k in range(4):
            slab = g * 4 + k
            xwp = xw_bf[:, slab * LANES:(slab + 1) * LANES]
            pair = []
            for h in (2 * slab, 2 * slab + 1):
                seg = a_cum[:, h:h + 1] - a_cum_t[h:h + 1, :q_pad]
                decay_in = jnp.exp(jnp.where(causal, seg, -jnp.inf))
                pair.append(jnp.dot((cb * decay_in).astype(BF16), xwp, preferred_element_type=F32))
            parts.append(jnp.where(lane < SSD_HEADDIM, pair[0], pair[1]))
        y_diag = jnp.concatenate(parts, axis=1)
        ys.append(y_diag + y_off * eac_x[:, g * GROUP_W:(g + 1) * GROUP_W])
        s_new = lax.dot_general(xwd_bf[:, g * GROUP_W:(g + 1) * GROUP_W], bg, _TN,
                                preferred_element_type=F32)
        h_scr[g * GROUP_W:(g + 1) * GROUP_W, :] = hg * dec[g * GROUP_W:(g + 1) * GROUP_W, :] + s_new

    y = jnp.concatenate(ys, axis=1) + dsk_ref[...] * xs
    gated = y * _silu(z_scr[...])
    outs = []
    for g in range(SSD_GROUPS):
        ug = gated[:, g * GROUP_W:(g + 1) * GROUP_W]
        outs.append(ug * lax.rsqrt(jnp.mean(ug * ug, axis=-1, keepdims=True) + EPS))
    out = jnp.concatenate(outs, axis=1) * ng_ref[...]
    y_ref[0] = out[0:q_real, :]

    @pl.when(c == nc - 1)
    def _():
        hout_ref[0] = h_scr[...]


def _ssd(p3, dt3, consts, state=None):
    b, L, _ = p3.shape
    q_real = min(SSD_CHUNK, L)
    q_pad = -(-q_real // SUBLANES) * SUBLANES
    nc = L // q_real
    has_init = state is not None
    cw, cbias, dtb, alog, dsk, ng, e01, et01 = consts

    def const(shape):
        return pl.BlockSpec(shape, lambda i, c: (0,) * len(shape))

    in_specs = [
        pl.BlockSpec((1, q_real, CONV_DIM), lambda i, c: (i, c, COL_XBC // CONV_DIM)),
        pl.BlockSpec((1, q_real, D_INNER), lambda i, c: (i, c, COL_Z // D_INNER)),
        pl.BlockSpec((1, q_real, LANES), lambda i, c: (i, c, 0)),
    ]
    args = [p3, p3, dt3]
    if has_init:
        h0, cbuf, layer = state
        in_specs += [
            pl.BlockSpec((1, 1, D_INNER, D_STATE), lambda i, c: (layer, i, 0, 0)),
            pl.BlockSpec((1, 1, CONV_W - 1, CONV_DIM), lambda i, c: (layer, i, 0, 0)),
        ]
        args += [h0, cbuf]
    in_specs += [const((CONV_W, CONV_DIM)), const((1, CONV_DIM)), const((1, LANES)), const((1, LANES)),
                 const((1, D_INNER)), const((1, D_INNER)), const((LANES, D_INNER)), const((D_INNER, LANES))]
    args += [cw, cbias, dtb, alog, dsk, ng, e01, et01]
    return pl.pallas_call(
        functools.partial(_ssd_kernel, q_real=q_real, q_pad=q_pad, nc=nc, has_init=has_init),
        grid=(b, nc),
        in_specs=in_specs,
        out_specs=[
            pl.BlockSpec((1, q_real, D_INNER), lambda i, c: (i, c, 0)),
            pl.BlockSpec((1, D_INNER, D_STATE), lambda i, c: (i, 0, 0)),
        ],
        out_shape=[jax.ShapeDtypeStruct((b, L, D_INNER), F32),
                   jax.ShapeDtypeStruct((b, D_INNER, D_STATE), F32)],
        scratch_shapes=[
            pltpu.VMEM((SUBLANES + q_pad, CONV_DIM), F32),
            pltpu.VMEM((D_INNER, D_STATE), F32),
            pltpu.VMEM((q_pad, LANES), F32),
            pltpu.VMEM((q_pad, D_INNER), F32),
        ],
        compiler_params=_params(("parallel", "arbitrary")),
        name="ssd_s" if has_init else "ssd_p",
    )(*args)


MOBA_P_HEADS = 4
GATE_ROWS = 2 * SUBLANES


def _moba_p_kernel(slopes_ref, q_ref, k_ref, v_ref, o_ref, kmh_scr, kml_scr, k_scr, vt_scr, bias_scr,
                   sel_scr, qs_scr, acc_scr, m_scr, l_scr, *, nb):
    hg = pl.program_id(1)
    j = pl.program_id(2)
    blk = MOBA_BLOCK
    G = MOBA_P_HEADS
    hd = ATT_HEAD_DIM

    @pl.when(j == 0)
    def _():
        krow = lax.broadcasted_iota(jnp.int32, (blk, blk), 0).astype(F32)
        for g in range(G):
            means = []
            for n in range(nb):
                kn = k_ref[0, n * blk:(n + 1) * blk, g * hd:(g + 1) * hd]
                means.append(jnp.sum(kn, axis=0, keepdims=True) * (1.0 / blk))
                k_scr[n, g] = kn.astype(BF16)
                vt_scr[n, g] = v_ref[0, n * blk:(n + 1) * blk, g * hd:(g + 1) * hd].T.astype(BF16)
            km = jnp.concatenate(means + [jnp.zeros((GATE_ROWS - nb, hd), F32)], axis=0)
            km_hi = km.astype(BF16)
            kmh_scr[g] = km_hi
            kml_scr[g] = (km - km_hi.astype(F32)).astype(BF16)
            bias_scr[g] = slopes_ref[hg * G + g] * krow

    rowid = lax.broadcasted_iota(jnp.int32, (SUBLANES, blk), 0)
    krow_i = lax.broadcasted_iota(jnp.int32, (blk, blk), 0)
    qcol_i = lax.broadcasted_iota(jnp.int32, (blk, blk), 1)
    causal = krow_i <= qcol_i
    scale = hd ** -0.5
    c_own = (j * blk).astype(F32)

    qs_all, gates, s_own = [], [], []
    for g in range(G):
        q = q_ref[0, :, g * hd:(g + 1) * hd]
        q_hi = q.astype(BF16)
        q_lo = (q - q_hi.astype(F32)).astype(BF16)
        qs = (q * scale).astype(BF16)
        qs_all.append(qs)
        gates.append((lax.dot_general(kmh_scr[g], q_hi, _NT, preferred_element_type=F32)
                      + lax.dot_general(kmh_scr[g], q_lo, _NT, preferred_element_type=F32)
                      + lax.dot_general(kml_scr[g], q_hi, _NT, preferred_element_type=F32))[0:SUBLANES, :])
        s_own.append(lax.dot_general(k_scr[j, g], qs, _NT, preferred_element_type=F32))
    sels, m0s, l0s, p_own = [], [], [], []
    for g in range(G):
        gate = jnp.where(rowid < j, gates[g], -jnp.inf)
        cnt = jnp.zeros((SUBLANES, blk), F32)
        for m in range(nb):
            gm = gate[m:m + 1, :]
            beats = (gm > gate) | ((gm == gate) & (m < rowid))
            cnt = cnt + jnp.where(beats, 1.0, 0.0)
        sels.append(jnp.where((cnt < MOBA_TOPK) & (rowid < j), 1.0, 0.0))
        s = jnp.where(causal, s_own[g] + bias_scr[g], -jnp.inf)
        smax = jnp.max(s, axis=0, keepdims=True)
        p = jnp.exp(s - smax)
        m0s.append(smax + slopes_ref[hg * G + g] * jnp.full((1, blk), c_own, F32))
        l0s.append(jnp.sum(p, axis=0, keepdims=True))
        p_own.append(p.astype(BF16))
    for g in range(G):
        acc_scr[g] = jnp.dot(vt_scr[j, g], p_own[g], preferred_element_type=F32)
        sel_scr[g], qs_scr[g], m_scr[g], l_scr[g] = sels[g], qs_all[g], m0s[g], l0s[g]

    def body(n, carry):
        c_n = jnp.full((1, blk), (n * blk).astype(F32), F32)
        scores = [lax.dot_general(k_scr[n, g], qs_scr[g], _NT, preferred_element_type=F32) for g in range(G)]
        stats, probs = [], []
        for g in range(G):
            c_blk = slopes_ref[hg * G + g] * c_n
            s = scores[g] + bias_scr[g]
            m_i = m_scr[g]
            m_use = jnp.maximum(m_i, jnp.max(s, axis=0, keepdims=True) + c_blk)
            p = jnp.exp(s - (m_use - c_blk))
            psum = jnp.sum(p, axis=0, keepdims=True)
            chosen = sel_scr[g, pl.ds(n, 1), :] > 0.0
            alpha = jnp.where(chosen, jnp.exp(m_i - m_use), 1.0)
            keep = jnp.where(chosen, 1.0, 0.0)
            stats.append((jnp.where(chosen, m_use, m_i), alpha * l_scr[g] + keep * psum, alpha, keep))
            probs.append(p.astype(BF16))
        for g in range(G):
            pv = jnp.dot(vt_scr[n, g], probs[g], preferred_element_type=F32)
            m_new, l_new, alpha, keep = stats[g]
            acc_scr[g] = alpha * acc_scr[g] + keep * pv
            m_scr[g], l_scr[g] = m_new, l_new
        return carry

    lax.fori_loop(0, j, body, 0)
    for g in range(G):
        o_ref[0, :, g * hd:(g + 1) * hd] = (acc_scr[g] / l_scr[g]).T


def _moba_p(p3, slopes):
    b, L, _ = p3.shape
    nb = L // MOBA_BLOCK
    G = MOBA_P_HEADS
    gw = G * ATT_HEAD_DIM
    assert nb <= SUBLANES and ATT_HEADS % G == 0 and COL_Q % gw == 0 and COL_K % gw == 0 and COL_V % gw == 0
    cq, ck, cv = COL_Q // gw, COL_K // gw, COL_V // gw
    return pl.pallas_call(
        functools.partial(_moba_p_kernel, nb=nb),
        grid=(b, ATT_HEADS // G, nb),
        in_specs=[
            pl.BlockSpec(memory_space=pltpu.SMEM),
            pl.BlockSpec((1, MOBA_BLOCK, gw), lambda i, h, j: (i, j, cq + h)),
            pl.BlockSpec((1, L, gw), lambda i, h, j: (i, 0, ck + h)),
            pl.BlockSpec((1, L, gw), lambda i, h, j: (i, 0, cv + h)),
        ],
        out_specs=pl.BlockSpec((1, MOBA_BLOCK, gw), lambda i, h, j: (i, j, h)),
        out_shape=jax.ShapeDtypeStruct((b, L, ATT_WIDTH), F32),
        scratch_shapes=[
            pltpu.VMEM((G, GATE_ROWS, ATT_HEAD_DIM), BF16),
            pltpu.VMEM((G, GATE_ROWS, ATT_HEAD_DIM), BF16),
            pltpu.VMEM((nb, G, MOBA_BLOCK, ATT_HEAD_DIM), BF16),
            pltpu.VMEM((nb, G, ATT_HEAD_DIM, MOBA_BLOCK), BF16),
            pltpu.VMEM((G, MOBA_BLOCK, MOBA_BLOCK), F32),
            pltpu.VMEM((G, SUBLANES, MOBA_BLOCK), F32),
            pltpu.VMEM((G, MOBA_BLOCK, ATT_HEAD_DIM), BF16),
            pltpu.VMEM((G, ATT_HEAD_DIM, MOBA_BLOCK), F32),
            pltpu.VMEM((G, 1, MOBA_BLOCK), F32),
            pltpu.VMEM((G, 1, MOBA_BLOCK), F32),
        ],
        compiler_params=_params(("parallel", "parallel", "arbitrary")),
        name="moba_p",
    )(slopes, p3, p3, p3)


def _moba_s_kernel(pt_ref, q_ref, kn_ref, vn_ref, k0_ref, k1_ref, v0_ref, v1_ref, o_ref,
                   acc_scr, m_scr, l_scr, g_scr, *, nb, lq, past_len):
    n = pl.program_id(1)
    rows = lq * ATT_HEADS
    keys = PAGE_SIZE * ATT_HEADS
    rid = lax.broadcasted_iota(jnp.int32, (rows, 1), 0)
    head = rid % ATT_HEADS
    t_pos = (past_len + rid // ATT_HEADS).astype(F32)
    slope = jnp.exp2(-(head + 1).astype(F32) * (8.0 / ATT_HEADS))
    scale = ATT_HEAD_DIM ** -0.5

    q = q_ref[0]
    q_bf = q.astype(BF16)
    k0 = k0_ref[0, 0]
    k1 = k1_ref[0, 0]
    kmean = (jnp.sum(k0, axis=0) + jnp.sum(k1, axis=0)) * (1.0 / MOBA_BLOCK)
    gate = jnp.sum(q * jnp.concatenate([kmean] * lq, axis=0), axis=-1, keepdims=True)

    col = lax.broadcasted_iota(jnp.int32, (rows, keys), 1)
    same_head = (col % ATT_HEADS) == head
    tok = col // ATT_HEADS

    def half_scores(kh, off):
        k2 = kh.reshape(keys, ATT_HEAD_DIM).astype(BF16)
        s = lax.dot_general(q_bf, k2, _NT, preferred_element_type=F32) * scale
        kpos = (n * MOBA_BLOCK + off + tok).astype(F32)
        return jnp.where(same_head, s - slope * (t_pos - kpos), -jnp.inf)

    s0 = half_scores(k0, 0)
    s1 = half_scores(k1, PAGE_SIZE)
    m_n = jnp.maximum(jnp.max(s0, axis=-1, keepdims=True), jnp.max(s1, axis=-1, keepdims=True))
    p0 = jnp.exp(s0 - m_n)
    p1 = jnp.exp(s1 - m_n)
    l_n = jnp.sum(p0, axis=-1, keepdims=True) + jnp.sum(p1, axis=-1, keepdims=True)
    v0 = v0_ref[0, 0].reshape(keys, ATT_HEAD_DIM).astype(BF16)
    v1 = v1_ref[0, 0].reshape(keys, ATT_HEAD_DIM).astype(BF16)
    acc_scr[n] = (jnp.dot(p0.astype(BF16), v0, preferred_element_type=F32)
                  + jnp.dot(p1.astype(BF16), v1, preferred_element_type=F32))
    m_scr[n] = jnp.broadcast_to(m_n, (rows, LANES))
    l_scr[n] = jnp.broadcast_to(l_n, (rows, LANES))
    g_scr[n] = jnp.broadcast_to(gate, (rows, LANES))

    @pl.when(n == nb - 1)
    def _():
        qidx = rid // ATT_HEADS
        s_own = []
        for kk in range(lq):
            kn = jnp.concatenate([kn_ref[0, kk * ATT_HEADS:(kk + 1) * ATT_HEADS, :]] * lq, axis=0)
            s = jnp.sum(q * kn, axis=-1, keepdims=True) * scale
            s = s - slope * (t_pos - float(past_len + kk))
            s_own.append(jnp.where(kk <= qidx, s, -jnp.inf))
        gates = [g_scr[i][:, 0:1] for i in range(nb)]
        ms = [m_scr[i][:, 0:1] for i in range(nb)]
        sel = []
        for i in range(nb):
            cnt = jnp.zeros((rows, 1), F32)
            for m in range(nb):
                beats = (gates[m] > gates[i]) | ((gates[m] == gates[i]) & (m < i))
                cnt = cnt + jnp.where(beats, 1.0, 0.0)
            sel.append(cnt < MOBA_TOPK)
        m_tot = s_own[0]
        for kk in range(1, lq):
            m_tot = jnp.maximum(m_tot, s_own[kk])
        for i in range(nb):
            m_tot = jnp.maximum(m_tot, jnp.where(sel[i], ms[i], -jnp.inf))
        l_tot = jnp.zeros((rows, 1), F32)
        acc = jnp.zeros((rows, ATT_HEAD_DIM), F32)
        for kk in range(lq):
            w = jnp.exp(s_own[kk] - m_tot)
            l_tot = l_tot + w
            acc = acc + w * jnp.concatenate([vn_ref[0, kk * ATT_HEADS:(kk + 1) * ATT_HEADS, :]] * lq, axis=0)
        for i in range(nb):
            w = jnp.where(sel[i], jnp.exp(ms[i] - m_tot), 0.0)
            l_tot = l_tot + w * l_scr[i][:, 0:1]
            acc = acc + w * acc_scr[i]
        o_ref[0] = acc / l_tot


def _moba_s(q2, kn2, vn2, cache_k, cache_v, page_table, layer):
    b, rows, _ = q2.shape
    lq = rows // ATT_HEADS
    n_pages = page_table.shape[1]
    past_len = n_pages * PAGE_SIZE
    nb = past_len // MOBA_BLOCK
    ppb = MOBA_BLOCK // PAGE_SIZE
    assert ppb == 2 and nb >= MOBA_TOPK and lq <= MOBA_BLOCK

    def page(off):
        return pl.BlockSpec((1, 1, PAGE_SIZE, ATT_HEADS, ATT_HEAD_DIM),
                            lambda i, n, pt: (layer, pt[i, ppb * n + off], 0, 0, 0))

    def new(_):
        return pl.BlockSpec((1, rows, ATT_HEAD_DIM), lambda i, n, pt: (i, 0, 0))

    grid_spec = pltpu.PrefetchScalarGridSpec(
        num_scalar_prefetch=1,
        grid=(b, nb),
        in_specs=[new(0), new(1), new(2), page(0), page(1), page(0), page(1)],
        out_specs=pl.BlockSpec((1, rows, ATT_HEAD_DIM), lambda i, n, pt: (i, 0, 0)),
        scratch_shapes=[
            pltpu.VMEM((nb, rows, ATT_HEAD_DIM), F32),
            pltpu.VMEM((nb, rows, LANES), F32),
            pltpu.VMEM((nb, rows, LANES), F32),
            pltpu.VMEM((nb, rows, LANES), F32),
        ],
    )
    return pl.pallas_call(
        functools.partial(_moba_s_kernel, nb=nb, lq=lq, past_len=past_len),
        grid_spec=grid_spec,
        out_shape=jax.ShapeDtypeStruct((b, rows, ATT_HEAD_DIM), F32),
        compiler_params=_params(("parallel", "arbitrary")),
        name="moba_s",
    )(page_table, q2, kn2, vn2, cache_k, cache_k, cache_v, cache_v)


def _merge_kernel(x_ref, ya_ref, yb_ref, ga_ref, gb_ref, wa_ref, wb_ref, wo_ref, o_ref):
    a = jnp.dot(ya_ref[...].astype(BF16), wa_ref[...], preferred_element_type=F32)
    b = jnp.dot(yb_ref[...].astype(BF16), wb_ref[...], preferred_element_type=F32)
    m = jax.nn.sigmoid(ga_ref[...]) * a + jax.nn.sigmoid(gb_ref[...]) * b
    o_ref[...] = x_ref[...] + jnp.dot(m.astype(BF16), wo_ref[...], preferred_element_type=F32)


def _merge(x, ya, yb, p, wpa, wpb, wo):
    t = x.shape[0]
    tm = min(512, t)
    return pl.pallas_call(
        _merge_kernel,
        grid=(t // tm,),
        in_specs=[
            pl.BlockSpec((tm, D_MODEL), lambda i: (i, 0)),
            pl.BlockSpec((tm, D_INNER), lambda i: (i, 0)),
            pl.BlockSpec((tm, ATT_WIDTH), lambda i: (i, 0)),
            pl.BlockSpec((tm, D_MODEL), lambda i: (i, COL_GA // D_MODEL)),
            pl.BlockSpec((tm, D_MODEL), lambda i: (i, COL_GB // D_MODEL)),
            pl.BlockSpec((D_INNER, D_MODEL), lambda i: (0, 0)),
            pl.BlockSpec((ATT_WIDTH, D_MODEL), lambda i: (0, 0)),
            pl.BlockSpec((D_MODEL, D_MODEL), lambda i: (0, 0)),
        ],
        out_specs=pl.BlockSpec((tm, D_MODEL), lambda i: (i, 0)),
        out_shape=jax.ShapeDtypeStruct((t, D_MODEL), F32),
        compiler_params=_params(("parallel",)),
        name="merge",
    )(x, ya, yb, p, p, wpa, wpb, wo)


def _norm_kernel(x_ref, g_ref, o_ref):
    o_ref[...] = _rms(x_ref[...], g_ref[...])


def _final_norm(x, g):
    t = x.shape[0]
    tm = min(512, t)
    return pl.pallas_call(
        _norm_kernel,
        grid=(t // tm,),
        in_specs=[pl.BlockSpec((tm, D_MODEL), lambda i: (i, 0)), pl.BlockSpec((1, D_MODEL), lambda i: (0, 0))],
        out_specs=pl.BlockSpec((tm, D_MODEL), lambda i: (i, 0)),
        out_shape=jax.ShapeDtypeStruct((t, D_MODEL), F32),
        compiler_params=_params(("parallel",)),
        name="final_norm",
    )(x, g)


def _kvout_kernel(*refs, depth):
    ok_ref, ov_ref = refs[2 * depth:]
    for l in range(depth):
        ok_ref[l] = refs[2 * l][...].reshape(ok_ref.shape[1:])
        ov_ref[l] = refs[2 * l + 1][...].reshape(ov_ref.shape[1:])


def _kvout(ps):
    depth = len(ps)
    t = ps[0].shape[0]
    tm = min(512, t)
    in_specs, args = [], []
    for p in ps:
        in_specs += [pl.BlockSpec((tm, ATT_WIDTH), lambda i: (i, COL_K // ATT_WIDTH)),
                     pl.BlockSpec((tm, ATT_WIDTH), lambda i: (i, COL_V // ATT_WIDTH))]
        args += [p, p]
    out_spec = pl.BlockSpec((depth, tm, ATT_HEADS, ATT_HEAD_DIM), lambda i: (0, i, 0, 0))
    out_shape = jax.ShapeDtypeStruct((depth, t, ATT_HEADS, ATT_HEAD_DIM), F32)
    return pl.pallas_call(
        functools.partial(_kvout_kernel, depth=depth),
        grid=(t // tm,),
        in_specs=in_specs,
        out_specs=[out_spec, out_spec],
        out_shape=[out_shape, out_shape],
        compiler_params=_params(("parallel",)),
        name="kvout",
    )(*args)


def _pad_lanes(v):
    return jnp.pad(v.astype(F32), (0, LANES - v.shape[0])).reshape(1, LANES)


def kernel(x_prompt, x_sample, cache_k, cache_v, state_ssm, state_conv, page_table, ln_ffn1, w_ffn1_in,
           w_ffn1_out, ln_mix, w_in, conv_w, conv_b, dt_bias, a_log, d_skip, ssd_norm, w_proj_a, w_proj_b,
           w_out, ln_ffn2, w_ffn2_in, w_ffn2_out, ln_final):
    depth = w_in.shape[0]
    bp, seq, _ = x_prompt.shape
    bs, lq, _ = x_sample.shape
    slopes = jnp.exp2(-8.0 * jnp.arange(1, ATT_HEADS + 1, dtype=F32) / ATT_HEADS)
    head_of = jnp.arange(D_INNER, dtype=jnp.int32) // SSD_HEADDIM
    e01 = (jnp.arange(LANES, dtype=jnp.int32)[:, None] == head_of[None, :]).astype(BF16)
    et01 = e01.T
    h0_all = state_ssm.reshape(depth, bs, D_INNER, D_STATE)

    o_z, o_xbc, o_dt = 0, D_INNER, D_INNER + CONV_DIM
    o_q = o_dt + SSD_HEADS
    o_k, o_v, o_ga, o_gb = o_q + ATT_WIDTH, o_q + 2 * ATT_WIDTH, o_q + 3 * ATT_WIDTH, o_q + 4 * ATT_WIDTH

    xp = x_prompt.reshape(bp * seq, D_MODEL)
    xs = x_sample.reshape(bs * lq, D_MODEL)
    outs = {k: [] for k in ("hp", "cp", "ks", "vs", "hs", "cs")}
    pps = []
    row = lambda v: v.astype(F32).reshape(1, -1)
    heads_rows = lambda a: a.reshape(bs, lq * ATT_HEADS, ATT_HEAD_DIM)
    for l in range(depth):
        wl = w_in[l]
        w_main = jnp.concatenate(
            [wl[:, o_z:o_xbc], wl[:, o_ga:o_gb], wl[:, o_xbc:o_dt], wl[:, o_q:o_ga], wl[:, o_gb:]],
            axis=1).astype(BF16)
        w_dt = jnp.pad(wl[:, o_dt:o_q], ((0, 0), (0, LANES - SSD_HEADS)))
        w_dt_hi = w_dt.astype(BF16)
        w_dt_lo = (w_dt - w_dt_hi.astype(F32)).astype(BF16)
        w1i, w1o = w_ffn1_in[l].astype(BF16), w_ffn1_out[l].astype(BF16)
        w2i, w2o = w_ffn2_in[l].astype(BF16), w_ffn2_out[l].astype(BF16)
        wpa, wpb, wo = w_proj_a[l].astype(BF16), w_proj_b[l].astype(BF16), w_out[l].astype(BF16)
        consts = (conv_w[l], row(conv_b[l]), _pad_lanes(dt_bias[l]), _pad_lanes(a_log[l]),
                  row(jnp.repeat(d_skip[l], SSD_HEADDIM)), row(ssd_norm[l]), e01, et01)
        g1, gm, g2 = row(ln_ffn1[l]), row(ln_mix[l]), row(ln_ffn2[l])

        xp = _ffn(xp, g1, w1i, w1o)
        xs = _ffn(xs, g1, w1i, w1o)
        pp, dtp = _inproj(xp, gm, w_main, w_dt_hi, w_dt_lo)
        ps, dts = _inproj(xs, gm, w_main, w_dt_hi, w_dt_lo)
        pp3 = pp.reshape(bp, seq, P_COLS)
        ps3 = ps.reshape(bs, lq, P_COLS)
        pps.append(pp)
        ks_new = ps3[:, :, COL_K:COL_K + ATT_WIDTH].reshape(bs, lq, ATT_HEADS, ATT_HEAD_DIM)
        vs_new = ps3[:, :, COL_V:COL_V + ATT_WIDTH].reshape(bs, lq, ATT_HEADS, ATT_HEAD_DIM)

        ya_p, h_p = _ssd(pp3, dtp.reshape(bp, seq, LANES), consts)
        ya_s, h_s = _ssd(ps3, dts.reshape(bs, lq, LANES), consts, state=(h0_all, state_conv, l))
        yb_p = _moba_p(pp3, slopes)
        yb_s = _moba_s(heads_rows(ps3[:, :, COL_Q:COL_Q + ATT_WIDTH]), heads_rows(ks_new), heads_rows(vs_new),
                       cache_k, cache_v, page_table, l)

        xp = _merge(xp, ya_p.reshape(bp * seq, D_INNER), yb_p.reshape(bp * seq, ATT_WIDTH), pp, wpa, wpb, wo)
        xs = _merge(xs, ya_s.reshape(bs * lq, D_INNER), yb_s.reshape(bs * lq, ATT_WIDTH), ps, wpa, wpb, wo)
        xp = _ffn(xp, g2, w2i, w2o)
        xs = _ffn(xs, g2, w2i, w2o)

        outs["hp"].append(h_p.reshape(bp, SSD_HEADS, SSD_HEADDIM, D_STATE))
        outs["cp"].append(pp3[:, seq - (CONV_W - 1):, COL_XBC:COL_XBC + CONV_DIM])
        outs["ks"].append(ks_new)
        outs["vs"].append(vs_new)
        outs["hs"].append(h_s.reshape(bs, SSD_HEADS, SSD_HEADDIM, D_STATE))
        if lq >= CONV_W - 1:
            cs = ps3[:, lq - (CONV_W - 1):, COL_XBC:COL_XBC + CONV_DIM]
        else:
            cs = jnp.concatenate([state_conv[l], ps3[:, :, COL_XBC:COL_XBC + CONV_DIM]], axis=1)[:, -(CONV_W - 1):]
        outs["cs"].append(cs)

    gf = row(ln_final)
    y_prompt = _final_norm(xp, gf).reshape(bp, seq, D_MODEL)
    y_sample = _final_norm(xs, gf).reshape(bs, lq, D_MODEL)
    k_prompt, v_prompt = _kvout(pps)
    kv5 = lambda a: a.reshape(depth, bp, seq, ATT_HEADS, ATT_HEAD_DIM)
    st = lambda k: jnp.stack(outs[k])
    return (y_prompt, y_sample, kv5(k_prompt), kv5(v_prompt), st("hp"), st("cp"), st("ks"), st("vs"), st("hs"),
            st("cs"))
```

```python
import functools

import jax
import jax.numpy as jnp
from jax import lax
from jax.experimental import pallas as pl
from jax.experimental.pallas import tpu as pltpu

F32 = jnp.float32
BF16 = jnp.bfloat16

D_MODEL = 1024
D_INNER = 2048
SSD_HEADDIM = 64
SSD_HEADS = 32
SSD_GROUPS = 4
GROUP_W = D_INNER // SSD_GROUPS
D_STATE = 128
CONV_W = 4
CONV_DIM = 3072
SSD_CHUNK = 128
ATT_HEADS = 8
ATT_HEAD_DIM = 128
ATT_WIDTH = 1024
MOBA_BLOCK = 256
MOBA_TOPK = 3
PAGE_SIZE = 128
D_FF = 2816
FFN_SCALE = 0.5
EPS = 1e-6

LANES = 128
SUBLANES = 8
VMEM_LIMIT = 48 * 1024 * 1024

P_COLS = 10240
COL_Z = 0
COL_GA = 2048
COL_XBC = 3072
COL_Q = 6144
COL_K = 7168
COL_V = 8192
COL_GB = 9216

_NT = (((1,), (1,)), ((), ()))
_TN = (((0,), (0,)), ((), ()))


def _params(sem):
    return pltpu.CompilerParams(dimension_semantics=sem, vmem_limit_bytes=VMEM_LIMIT)


def _rms(x, g):
    return x * lax.rsqrt(jnp.mean(x * x, axis=-1, keepdims=True) + EPS) * g


def _silu(x):
    return x * jax.nn.sigmoid(x)


def _split3(x):
    x1 = x.astype(BF16)
    r = x - x1.astype(F32)
    x2 = r.astype(BF16)
    r = r - x2.astype(F32)
    return x1, x2, r.astype(BF16)


def _sel_left(a01, x, pieces=3):
    return sum(jnp.dot(a01, p, preferred_element_type=F32) for p in _split3(x)[:pieces])


def _sel_right(x, e01, pieces=3):
    return sum(jnp.dot(p, e01, preferred_element_type=F32) for p in _split3(x)[:pieces])


def _ffn_kernel(x_ref, g_ref, wa_ref, wb_ref, wo_ref, gout_ref, o_ref, h_scr, acc_scr, *, nf, norm_out):
    f = pl.program_id(1)

    @pl.when(f == 0)
    def _():
        h_scr[...] = _rms(x_ref[...], g_ref[...]).astype(BF16)
        acc_scr[...] = jnp.zeros_like(acc_scr)

    h = h_scr[...]
    a = jnp.dot(h, wa_ref[...], preferred_element_type=F32)
    b = jnp.dot(h, wb_ref[...], preferred_element_type=F32)
    acc_scr[...] += jnp.dot((_silu(a) * b).astype(BF16), wo_ref[...], preferred_element_type=F32)

    @pl.when(f == nf - 1)
    def _():
        y = x_ref[...] + FFN_SCALE * acc_scr[...]
        o_ref[...] = _rms(y, gout_ref[...]) if norm_out else y


def _ffn(x, g, w_in, w_out, g_out=None):
    t = x.shape[0]
    tm = min(512, t)
    nf = 2
    tf = D_FF // nf
    norm_out = g_out is not None
    return pl.pallas_call(
        functools.partial(_ffn_kernel, nf=nf, norm_out=norm_out),
        grid=(t // tm, nf),
        in_specs=[
            pl.BlockSpec((tm, D_MODEL), lambda i, f: (i, 0)),
            pl.BlockSpec((1, D_MODEL), lambda i, f: (0, 0)),
            pl.BlockSpec((D_MODEL, tf), lambda i, f: (0, f)),
            pl.BlockSpec((D_MODEL, tf), lambda i, f: (0, f + nf)),
            pl.BlockSpec((tf, D_MODEL), lambda i, f: (f, 0)),
            pl.BlockSpec((1, D_MODEL), lambda i, f: (0, 0)),
        ],
        out_specs=pl.BlockSpec((tm, D_MODEL), lambda i, f: (i, 0)),
        out_shape=jax.ShapeDtypeStruct((t, D_MODEL), F32),
        scratch_shapes=[pltpu.VMEM((tm, D_MODEL), BF16), pltpu.VMEM((tm, D_MODEL), F32)],
        compiler_params=_params(("parallel", "arbitrary")),
        name="ffn",
    )(x, g, w_in, w_in, w_out, g_out if norm_out else g)


def _inproj_kernel(x_ref, g_ref, w_ref, wdth_ref, wdtl_ref, p_ref, dt_ref, h_scr):
    n = pl.program_id(1)

    @pl.when(n == 0)
    def _():
        h = _rms(x_ref[...], g_ref[...])
        h_hi = h.astype(BF16)
        h_lo = (h - h_hi.astype(F32)).astype(BF16)
        h_scr[...] = h_hi
        dt_ref[...] = (jnp.dot(h_hi, wdth_ref[...], preferred_element_type=F32)
                       + jnp.dot(h_hi, wdtl_ref[...], preferred_element_type=F32)
                       + jnp.dot(h_lo, wdth_ref[...], preferred_element_type=F32))

    p_ref[...] = jnp.dot(h_scr[...], w_ref[...], preferred_element_type=F32)


def _inproj(x, g, w_main, w_dt_hi, w_dt_lo):
    t = x.shape[0]
    tm = 1024 if t % 1024 == 0 else min(512, t)
    tn = 1024
    return pl.pallas_call(
        _inproj_kernel,
        grid=(t // tm, P_COLS // tn),
        in_specs=[
            pl.BlockSpec((tm, D_MODEL), lambda i, n: (i, 0)),
            pl.BlockSpec((1, D_MODEL), lambda i, n: (0, 0)),
            pl.BlockSpec((D_MODEL, tn), lambda i, n: (0, n)),
            pl.BlockSpec((D_MODEL, LANES), lambda i, n: (0, 0)),
            pl.BlockSpec((D_MODEL, LANES), lambda i, n: (0, 0)),
        ],
        out_specs=[
            pl.BlockSpec((tm, tn), lambda i, n: (i, n)),
            pl.BlockSpec((tm, LANES), lambda i, n: (i, 0)),
        ],
        out_shape=[jax.ShapeDtypeStruct((t, P_COLS), F32), jax.ShapeDtypeStruct((t, LANES), F32)],
        scratch_shapes=[pltpu.VMEM((tm, D_MODEL), BF16)],
        compiler_params=_params(("parallel", "arbitrary")),
        name="inproj",
    )(x, g, w_main, w_dt_hi, w_dt_lo)


_HIST = SUBLANES - (CONV_W - 1)


def _ssd_kernel(*refs, q_real, q_pad, nc, has_init):
    if has_init:
        (xbc_ref, z_ref, dt_ref, h0_ref, cbuf_ref, cw_ref, cb_ref, dtb_ref, alog_ref, dsk_ref,
         ng_ref, e_ref, et_ref, y_ref, hout_ref, ext_scr, h_scr, dt_scr, z_scr) = refs
    else:
        (xbc_ref, z_ref, dt_ref, cw_ref, cb_ref, dtb_ref, alog_ref, dsk_ref,
         ng_ref, e_ref, et_ref, y_ref, hout_ref, ext_scr, h_scr, dt_scr, z_scr) = refs
    c = pl.program_id(1)

    @pl.when(c == 0)
    def _():
        ext_scr[...] = jnp.zeros_like(ext_scr)
        dt_scr[...] = jnp.zeros_like(dt_scr)
        z_scr[...] = jnp.zeros_like(z_scr)
        if has_init:
            h_scr[...] = h0_ref[0, 0]
            ext_scr[_HIST:SUBLANES, :] = cbuf_ref[0, 0]
        else:
            h_scr[...] = jnp.zeros_like(h_scr)

    ext_scr[SUBLANES:SUBLANES + q_real, :] = xbc_ref[0]
    acc = cb_ref[...] + cw_ref[0:1, :] * ext_scr[_HIST:_HIST + q_pad, :]
    for w in range(1, CONV_W):
        acc = acc + cw_ref[w:w + 1, :] * ext_scr[_HIST + w:_HIST + w + q_pad, :]
    if nc > 1:
        ext_scr[_HIST:SUBLANES, :] = ext_scr[SUBLANES + q_real - (CONV_W - 1):SUBLANES + q_real, :]
    u = _silu(acc)
    xs = u[:, :D_INNER]
    b_bf = u[:, D_INNER:D_INNER + SSD_GROUPS * D_STATE].astype(BF16)
    c_bf = u[:, D_INNER + SSD_GROUPS * D_STATE:].astype(BF16)

    dt_scr[0:q_real, :] = dt_ref[0]
    z_scr[0:q_real, :] = z_ref[0]
    rows = lax.broadcasted_iota(jnp.int32, (q_pad, LANES), 0)
    dtr = dt_scr[...] + dtb_ref[...]
    softplus = jnp.maximum(dtr, 0.0) + jnp.log1p(jnp.exp(-jnp.abs(dtr)))
    dt = jnp.where(rows < q_real, softplus, 0.0)
    da = dt * (-jnp.exp(alog_ref[...]))

    ri = lax.broadcasted_iota(jnp.int32, (q_pad, q_pad), 0)
    ci = lax.broadcasted_iota(jnp.int32, (q_pad, q_pad), 1)
    causal = ri >= ci
    a_cum = _sel_left(causal.astype(BF16), da)
    a_last = a_cum[q_pad - 1:q_pad, :]
    if q_pad < LANES:
        a_sq = jnp.concatenate([a_cum, jnp.zeros((LANES - q_pad, LANES), F32)], axis=0)
    else:
        a_sq = a_cum
    a_cum_t = a_sq.T

    e01 = e_ref[...]
    dt_x = _sel_right(dt, e01, pieces=2)
    eac_x = _sel_right(jnp.exp(a_cum), e01, pieces=2)
    dte_x = _sel_right(jnp.exp(a_last - a_cum), e01, pieces=2)
    xw = xs * dt_x
    xw_bf = xw.astype(BF16)
    xwd_bf = (xw * dte_x).astype(BF16)

    dec_rows = jnp.broadcast_to(jnp.exp(a_cum_t[:, q_pad - 1:q_pad]), (LANES, LANES))
    dec = _sel_left(et_ref[...], dec_rows, pieces=2)

    lane = lax.broadcasted_iota(jnp.int32, (q_pad, LANES), 1)
    ys = []
    for g in range(SSD_GROUPS):
        cg = c_bf[:, g * D_STATE:(g + 1) * D_STATE]
        bg = b_bf[:, g * D_STATE:(g + 1) * D_STATE]
        cb = lax.dot_general(cg, bg, _NT, preferred_element_type=F32)
        hg = h_scr[g * GROUP_W:(g + 1) * GROUP_W, :]
        y_off = lax.dot_general(cg, hg.astype(BF16), _NT, preferred_element_type=F32)
        parts = []
        for k in range(4):
            slab = g * 4 + k
            xwp = xw_bf[:, slab * LANES:(slab + 1) * LANES]
            pair = []
            for h in (2 * slab, 2 * slab + 1):
                seg = a_cum[:, h:h + 1] - a_cum_t[h:h + 1, :q_pad]
                decay_in = jnp.exp(jnp.where(causal, seg, -jnp.inf))
                pair.append(jnp.dot((cb * decay_in).astype(BF16), xwp, preferred_element_type=F32))
            parts.append(jnp.where(lane < SSD_HEADDIM, pair[0], pair[1]))
        y_diag = jnp.concatenate(parts, axis=1)
        ys.append(y_diag + y_off * eac_x[:, g * GROUP_W:(g + 1) * GROUP_W])
        s_new = lax.dot_general(xwd_bf[:, g * GROUP_W:(g + 1) * GROUP_W], bg, _TN,
                                preferred_element_type=F32)
        h_scr[g * GROUP_W:(g + 1) * GROUP_W, :] = hg * dec[g * GROUP_W:(g + 1) * GROUP_W, :] + s_new

    y = jnp.concatenate(ys, axis=1) + dsk_ref[...] * xs
    gated = y * _silu(z_scr[...])
    outs = []
    for g in range(SSD_GROUPS):
        ug = gated[:, g * GROUP_W:(g + 1) * GROUP_W]
        outs.append(ug * lax.rsqrt(jnp.mean(ug * ug, axis=-1, keepdims=True) + EPS))
    out = jnp.concatenate(outs, axis=1) * ng_ref[...]
    y_ref[0] = out[0:q_real, :]

    @pl.when(c == nc - 1)
    def _():
        hout_ref[0] = h_scr[...]


def _ssd(p3, dt3, consts, state=None):
    b, L, _ = p3.shape
    q_real = min(SSD_CHUNK, L)
    q_pad = -(-q_real // SUBLANES) * SUBLANES
    nc = L // q_real
    has_init = state is not None
    cw, cbias, dtb, alog, dsk, ng, e01, et01 = consts

    def const(shape):
        return pl.BlockSpec(shape, lambda i, c: (0,) * len(shape))

    in_specs = [
        pl.BlockSpec((1, q_real, CONV_DIM), lambda i, c: (i, c, COL_XBC // CONV_DIM)),
        pl.BlockSpec((1, q_real, D_INNER), lambda i, c: (i, c, COL_Z // D_INNER)),
        pl.BlockSpec((1, q_real, LANES), lambda i, c: (i, c, 0)),
    ]
    args = [p3, p3, dt3]
    if has_init:
        h0, cbuf, layer = state
        in_specs += [
            pl.BlockSpec((1, 1, D_INNER, D_STATE), lambda i, c: (layer, i, 0, 0)),
            pl.BlockSpec((1, 1, CONV_W - 1, CONV_DIM), lambda i, c: (layer, i, 0, 0)),
        ]
        args += [h0, cbuf]
    in_specs += [const((CONV_W, CONV_DIM)), const((1, CONV_DIM)), const((1, LANES)), const((1, LANES)),
                 const((1, D_INNER)), const((1, D_INNER)), const((LANES, D_INNER)), const((D_INNER, LANES))]
    args += [cw, cbias, dtb, alog, dsk, ng, e01, et01]
    return pl.pallas_call(
        functools.partial(_ssd_kernel, q_real=q_real, q_pad=q_pad, nc=nc, has_init=has_init),
        grid=(b, nc),
        in_specs=in_specs,
        out_specs=[
            pl.BlockSpec((1, q_real, D_INNER), lambda i, c: (i, c, 0)),
            pl.BlockSpec((1, D_INNER, D_STATE), lambda i, c: (i, 0, 0)),
        ],
        out_shape=[jax.ShapeDtypeStruct((b, L, D_INNER), F32),
                   jax.ShapeDtypeStruct((b, D_INNER, D_STATE), F32)],
        scratch_shapes=[
            pltpu.VMEM((SUBLANES + q_pad, CONV_DIM), F32),
            pltpu.VMEM((D_INNER, D_STATE), F32),
            pltpu.VMEM((q_pad, LANES), F32),
            pltpu.VMEM((q_pad, D_INNER), F32),
        ],
        compiler_params=_params(("parallel", "arbitrary")),
        name="ssd_s" if has_init else "ssd_p",
    )(*args)


MOBA_P_HEADS = 4
GATE_ROWS = 2 * SUBLANES


def _moba_p_kernel(slopes_ref, q_ref, k_ref, v_ref, o_ref, kmh_scr, kml_scr, k_scr, vt_scr, bias_scr,
                   sel_scr, qs_scr, acc_scr, m_scr, l_scr, *, nb):
    hg = pl.program_id(1)
    j = pl.program_id(2)
    blk = MOBA_BLOCK
    G = MOBA_P_HEADS
    hd = ATT_HEAD_DIM

    @pl.when(j == 0)
    def _():
        krow = lax.broadcasted_iota(jnp.int32, (blk, blk), 0).astype(F32)
        for g in range(G):
            means = []
            for n in range(nb):
                kn = k_ref[0, n * blk:(n + 1) * blk, g * hd:(g + 1) * hd]
                means.append(jnp.sum(kn, axis=0, keepdims=True) * (1.0 / blk))
                k_scr[n, g] = kn.astype(BF16)
                vt_scr[n, g] = v_ref[0, n * blk:(n + 1) * blk, g * hd:(g + 1) * hd].T.astype(BF16)
            km = jnp.concatenate(means + [jnp.zeros((GATE_ROWS - nb, hd), F32)], axis=0)
            km_hi = km.astype(BF16)
            kmh_scr[g] = km_hi
            kml_scr[g] = (km - km_hi.astype(F32)).astype(BF16)
            bias_scr[g] = slopes_ref[hg * G + g] * krow

    rowid = lax.broadcasted_iota(jnp.int32, (SUBLANES, blk), 0)
    krow_i = lax.broadcasted_iota(jnp.int32, (blk, blk), 0)
    qcol_i = lax.broadcasted_iota(jnp.int32, (blk, blk), 1)
    causal = krow_i <= qcol_i
    scale = hd ** -0.5
    c_own = (j * blk).astype(F32)

    qs_all, gates, s_own = [], [], []
    for g in range(G):
        q = q_ref[0, :, g * hd:(g + 1) * hd]
        q_hi = q.astype(BF16)
        q_lo = (q - q_hi.astype(F32)).astype(BF16)
        qs = (q * scale).astype(BF16)
        qs_all.append(qs)
        gates.append((lax.dot_general(kmh_scr[g], q_hi, _NT, preferred_element_type=F32)
                      + lax.dot_general(kmh_scr[g], q_lo, _NT, preferred_element_type=F32)
                      + lax.dot_general(kml_scr[g], q_hi, _NT, preferred_element_type=F32))[0:SUBLANES, :])
        s_own.append(lax.dot_general(k_scr[j, g], qs, _NT, preferred_element_type=F32))
    sels, m0s, l0s, p_own = [], [], [], []
    for g in range(G):
        gate = jnp.where(rowid < j, gates[g], -jnp.inf)
        cnt = jnp.zeros((SUBLANES, blk), F32)
        for m in range(nb):
            gm = gate[m:m + 1, :]
            beats = (gm > gate) | ((gm == gate) & (m < rowid))
            cnt = cnt + jnp.where(beats, 1.0, 0.0)
        sels.append(jnp.where((cnt < MOBA_TOPK) & (rowid < j), 1.0, 0.0))
        s = jnp.where(causal, s_own[g] + bias_scr[g], -jnp.inf)
        smax = jnp.max(s, axis=0, keepdims=True)
        p = jnp.exp(s - smax)
        m0s.append(smax + slopes_ref[hg * G + g] * jnp.full((1, blk), c_own, F32))
        l0s.append(jnp.sum(p, axis=0, keepdims=True))
        p_own.append(p.astype(BF16))
    for g in range(G):
        acc_scr[g] = jnp.dot(vt_scr[j, g], p_own[g], preferred_element_type=F32)
        sel_scr[g], qs_scr[g], m_scr[g], l_scr[g] = sels[g], qs_all[g], m0s[g], l0s[g]

    def body(n, carry):
        c_n = jnp.full((1, blk), (n * blk).astype(F32), F32)
        scores = [lax.dot_general(k_scr[n, g], qs_scr[g], _NT, preferred_element_type=F32) for g in range(G)]
        stats, probs = [], []
        for g in range(G):
            c_blk = slopes_ref[hg * G + g] * c_n
            s = scores[g] + bias_scr[g]
            m_i = m_scr[g]
            m_use = jnp.maximum(m_i, jnp.max(s, axis=0, keepdims=True) + c_blk)
            p = jnp.exp(s - (m_use - c_blk))
            psum = jnp.sum(p, axis=0, keepdims=True)
            chosen = sel_scr[g, pl.ds(n, 1), :] > 0.0
            alpha = jnp.where(chosen, jnp.exp(m_i - m_use), 1.0)
            keep = jnp.where(chosen, 1.0, 0.0)
            stats.append((jnp.where(chosen, m_use, m_i), alpha * l_scr[g] + keep * psum, alpha, keep))
            probs.append(p.astype(BF16))
        for g in range(G):
            pv = jnp.dot(vt_scr[n, g], probs[g], preferred_element_type=F32)
            m_new, l_new, alpha, keep = stats[g]
            acc_scr[g] = alpha * acc_scr[g] + keep * pv
            m_scr[g], l_scr[g] = m_new, l_new
        return carry

    lax.fori_loop(0, j, body, 0)
    for g in range(G):
        o_ref[0, :, g * hd:(g + 1) * hd] = (acc_scr[g] / l_scr[g]).T


def _moba_p(p3, slopes):
    b, L, _ = p3.shape
    nb = L // MOBA_BLOCK
    G = MOBA_P_HEADS
    gw = G * ATT_HEAD_DIM
    assert nb <= SUBLANES and ATT_HEADS % G == 0 and COL_Q % gw == 0 and COL_K % gw == 0 and COL_V % gw == 0
    cq, ck, cv = COL_Q // gw, COL_K // gw, COL_V // gw
    return pl.pallas_call(
        functools.partial(_moba_p_kernel, nb=nb),
        grid=(b, ATT_HEADS // G, nb),
        in_specs=[
            pl.BlockSpec(memory_space=pltpu.SMEM),
            pl.BlockSpec((1, MOBA_BLOCK, gw), lambda i, h, j: (i, j, cq + h)),
            pl.BlockSpec((1, L, gw), lambda i, h, j: (i, 0, ck + h)),
            pl.BlockSpec((1, L, gw), lambda i, h, j: (i, 0, cv + h)),
        ],
        out_specs=pl.BlockSpec((1, MOBA_BLOCK, gw), lambda i, h, j: (i, j, h)),
        out_shape=jax.ShapeDtypeStruct((b, L, ATT_WIDTH), F32),
        scratch_shapes=[
            pltpu.VMEM((G, GATE_ROWS, ATT_HEAD_DIM), BF16),
            pltpu.VMEM((G, GATE_ROWS, ATT_HEAD_DIM), BF16),
            pltpu.VMEM((nb, G, MOBA_BLOCK, ATT_HEAD_DIM), BF16),
            pltpu.VMEM((nb, G, ATT_HEAD_DIM, MOBA_BLOCK), BF16),
            pltpu.VMEM((G, MOBA_BLOCK, MOBA_BLOCK), F32),
            pltpu.VMEM((G, SUBLANES, MOBA_BLOCK), F32),
            pltpu.VMEM((G, MOBA_BLOCK, ATT_HEAD_DIM), BF16),
            pltpu.VMEM((G, ATT_HEAD_DIM, MOBA_BLOCK), F32),
            pltpu.VMEM((G, 1, MOBA_BLOCK), F32),
            pltpu.VMEM((G, 1, MOBA_BLOCK), F32),
        ],
        compiler_params=_params(("parallel", "parallel", "arbitrary")),
        name="moba_p",
    )(slopes, p3, p3, p3)


MOBA_S_BLOCKS = 4


def _moba_s_kernel(pt_ref, q_ref, kn_ref, vn_ref, *rest, nb, lq, past_len):
    npg = 2 * MOBA_S_BLOCKS
    k_refs, v_refs = rest[:npg], rest[npg:2 * npg]
    o_ref, acc_scr, m_scr, l_scr, g_scr = rest[2 * npg:]
    step = pl.program_id(1)
    rows = lq * ATT_HEADS
    keys = PAGE_SIZE * ATT_HEADS
    rid = lax.broadcasted_iota(jnp.int32, (rows, 1), 0)
    head = rid % ATT_HEADS
    t_pos = (past_len + rid // ATT_HEADS).astype(F32)
    slope = jnp.exp2(-(head + 1).astype(F32) * (8.0 / ATT_HEADS))
    scale = ATT_HEAD_DIM ** -0.5

    q = q_ref[0]
    q_bf = (q * scale).astype(BF16)
    col = lax.broadcasted_iota(jnp.int32, (rows, keys), 1)
    in_page = jnp.where((col % ATT_HEADS) == head, slope * ((col // ATT_HEADS).astype(F32) - t_pos), -jnp.inf)

    raw, gates = [], []
    for i in range(MOBA_S_BLOCKS):
        k0 = k_refs[2 * i][0, 0]
        k1 = k_refs[2 * i + 1][0, 0]
        kmean = (jnp.sum(k0, axis=0) + jnp.sum(k1, axis=0)) * (1.0 / MOBA_BLOCK)
        gates.append(jnp.sum(q * jnp.concatenate([kmean] * lq, axis=0), axis=-1, keepdims=True))
        raw.append([lax.dot_general(q_bf, kh.reshape(keys, ATT_HEAD_DIM).astype(BF16), _NT,
                                    preferred_element_type=F32) for kh in (k0, k1)])
    stats, probs = [], []
    for i in range(MOBA_S_BLOCKS):
        n = step * MOBA_S_BLOCKS + i
        s = [raw[i][h] + in_page + slope * (n * MOBA_BLOCK + h * PAGE_SIZE).astype(F32) for h in range(2)]
        m_n = jnp.maximum(jnp.max(s[0], axis=-1, keepdims=True), jnp.max(s[1], axis=-1, keepdims=True))
        p = [jnp.exp(s[h] - m_n) for h in range(2)]
        l_n = jnp.sum(p[0], axis=-1, keepdims=True) + jnp.sum(p[1], axis=-1, keepdims=True)
        stats.append((m_n, l_n))
        probs.append([ph.astype(BF16) for ph in p])
    for i in range(MOBA_S_BLOCKS):
        n = step * MOBA_S_BLOCKS + i
        acc_scr[n] = sum(jnp.dot(probs[i][h], v_refs[2 * i + h][0, 0].reshape(keys, ATT_HEAD_DIM).astype(BF16),
                                 preferred_element_type=F32) for h in range(2))
        m_scr[n] = jnp.broadcast_to(stats[i][0], (rows, LANES))
        l_scr[n] = jnp.broadcast_to(stats[i][1], (rows, LANES))
        g_scr[n] = jnp.broadcast_to(gates[i], (rows, LANES))

    @pl.when(step == nb // MOBA_S_BLOCKS - 1)
    def _():
        qidx = rid // ATT_HEADS
        s_own = []
        for kk in range(lq):
            kn = jnp.concatenate([kn_ref[0, kk * ATT_HEADS:(kk + 1) * ATT_HEADS, :]] * lq, axis=0)
            s = jnp.sum(q * kn, axis=-1, keepdims=True) * scale
            s = s - slope * (t_pos - float(past_len + kk))
            s_own.append(jnp.where(kk <= qidx, s, -jnp.inf))
        gates = [g_scr[i][:, 0:1] for i in range(nb)]
        ms = [m_scr[i][:, 0:1] for i in range(nb)]
        sel = []
        for i in range(nb):
            cnt = jnp.zeros((rows, 1), F32)
            for m in range(nb):
                beats = (gates[m] > gates[i]) | ((gates[m] == gates[i]) & (m < i))
                cnt = cnt + jnp.where(beats, 1.0, 0.0)
            sel.append(cnt < MOBA_TOPK)
        m_tot = s_own[0]
        for kk in range(1, lq):
            m_tot = jnp.maximum(m_tot, s_own[kk])
        for i in range(nb):
            m_tot = jnp.maximum(m_tot, jnp.where(sel[i], ms[i], -jnp.inf))
        l_tot = jnp.zeros((rows, 1), F32)
        acc = jnp.zeros((rows, ATT_HEAD_DIM), F32)
        for kk in range(lq):
            w = jnp.exp(s_own[kk] - m_tot)
            l_tot = l_tot + w
            acc = acc + w * jnp.concatenate([vn_ref[0, kk * ATT_HEADS:(kk + 1) * ATT_HEADS, :]] * lq, axis=0)
        for i in range(nb):
            w = jnp.where(sel[i], jnp.exp(ms[i] - m_tot), 0.0)
            l_tot = l_tot + w * l_scr[i][:, 0:1]
            acc = acc + w * acc_scr[i]
        o_ref[0] = acc / l_tot


def _moba_s(q2, kn2, vn2, cache_k, cache_v, page_table, layer):
    b, rows, _ = q2.shape
    lq = rows // ATT_HEADS
    n_pages = page_table.shape[1]
    past_len = n_pages * PAGE_SIZE
    nb = past_len // MOBA_BLOCK
    ppb = MOBA_BLOCK // PAGE_SIZE
    assert ppb == 2 and nb >= MOBA_TOPK and lq <= MOBA_BLOCK and nb % MOBA_S_BLOCKS == 0
    npg = ppb * MOBA_S_BLOCKS

    def page(off):
        return pl.BlockSpec((1, 1, PAGE_SIZE, ATT_HEADS, ATT_HEAD_DIM),
                            lambda i, n, pt: (layer, pt[i, npg * n + off], 0, 0, 0))

    def new(_):
        return pl.BlockSpec((1, rows, ATT_HEAD_DIM), lambda i, n, pt: (i, 0, 0))

    pages = [page(off) for off in range(npg)]
    grid_spec = pltpu.PrefetchScalarGridSpec(
        num_scalar_prefetch=1,
        grid=(b, nb // MOBA_S_BLOCKS),
        in_specs=[new(0), new(1), new(2)] + pages + pages,
        out_specs=pl.BlockSpec((1, rows, ATT_HEAD_DIM), lambda i, n, pt: (i, 0, 0)),
        scratch_shapes=[
            pltpu.VMEM((nb, rows, ATT_HEAD_DIM), F32),
            pltpu.VMEM((nb, rows, LANES), F32),
            pltpu.VMEM((nb, rows, LANES), F32),
            pltpu.VMEM((nb, rows, LANES), F32),
        ],
    )
    return pl.pallas_call(
        functools.partial(_moba_s_kernel, nb=nb, lq=lq, past_len=past_len),
        grid_spec=grid_spec,
        out_shape=jax.ShapeDtypeStruct((b, rows, ATT_HEAD_DIM), F32),
        compiler_params=_params(("parallel", "arbitrary")),
        name="moba_s",
    )(page_table, q2, kn2, vn2, *([cache_k] * npg), *([cache_v] * npg))


def _merge_kernel(x_ref, ya_ref, yb_ref, ga_ref, gb_ref, wa_ref, wb_ref, wo_ref, o_ref):
    a = jnp.dot(ya_ref[...].astype(BF16), wa_ref[...], preferred_element_type=F32)
    b = jnp.dot(yb_ref[...].astype(BF16), wb_ref[...], preferred_element_type=F32)
    m = jax.nn.sigmoid(ga_ref[...]) * a + jax.nn.sigmoid(gb_ref[...]) * b
    o_ref[...] = x_ref[...] + jnp.dot(m.astype(BF16), wo_ref[...], preferred_element_type=F32)


def _merge(x, ya, yb, p, wpa, wpb, wo):
    t = x.shape[0]
    tm = min(512, t)
    return pl.pallas_call(
        _merge_kernel,
        grid=(t // tm,),
        in_specs=[
            pl.BlockSpec((tm, D_MODEL), lambda i: (i, 0)),
            pl.BlockSpec((tm, D_INNER), lambda i: (i, 0)),
            pl.BlockSpec((tm, ATT_WIDTH), lambda i: (i, 0)),
            pl.BlockSpec((tm, D_MODEL), lambda i: (i, COL_GA // D_MODEL)),
            pl.BlockSpec((tm, D_MODEL), lambda i: (i, COL_GB // D_MODEL)),
            pl.BlockSpec((D_INNER, D_MODEL), lambda i: (0, 0)),
            pl.BlockSpec((ATT_WIDTH, D_MODEL), lambda i: (0, 0)),
            pl.BlockSpec((D_MODEL, D_MODEL), lambda i: (0, 0)),
        ],
        out_specs=pl.BlockSpec((tm, D_MODEL), lambda i: (i, 0)),
        out_shape=jax.ShapeDtypeStruct((t, D_MODEL), F32),
        compiler_params=_params(("parallel",)),
        name="merge",
    )(x, ya, yb, p, p, wpa, wpb, wo)


def _kvout_kernel(*refs, depth):
    ok_ref, ov_ref = refs[2 * depth:]
    for l in range(depth):
        ok_ref[l] = refs[2 * l][...].reshape(ok_ref.shape[1:])
        ov_ref[l] = refs[2 * l + 1][...].reshape(ov_ref.shape[1:])


def _kvout(ps):
    depth = len(ps)
    t = ps[0].shape[0]
    tm = min(512, t)
    in_specs, args = [], []
    for p in ps:
        in_specs += [pl.BlockSpec((tm, ATT_WIDTH), lambda i: (i, COL_K // ATT_WIDTH)),
                     pl.BlockSpec((tm, ATT_WIDTH), lambda i: (i, COL_V // ATT_WIDTH))]
        args += [p, p]
    out_spec = pl.BlockSpec((depth, tm, ATT_HEADS, ATT_HEAD_DIM), lambda i: (0, i, 0, 0))
    out_shape = jax.ShapeDtypeStruct((depth, t, ATT_HEADS, ATT_HEAD_DIM), F32)
    return pl.pallas_call(
        functools.partial(_kvout_kernel, depth=depth),
        grid=(t // tm,),
        in_specs=in_specs,
        out_specs=[out_spec, out_spec],
        out_shape=[out_shape, out_shape],
        compiler_params=_params(("parallel",)),
        name="kvout",
    )(*args)


def _pad_lanes(v):
    return jnp.pad(v.astype(F32), (0, LANES - v.shape[0])).reshape(1, LANES)


def kernel(x_prompt, x_sample, cache_k, cache_v, state_ssm, state_conv, page_table, ln_ffn1, w_ffn1_in,
           w_ffn1_out, ln_mix, w_in, conv_w, conv_b, dt_bias, a_log, d_skip, ssd_norm, w_proj_a, w_proj_b,
           w_out, ln_ffn2, w_ffn2_in, w_ffn2_out, ln_final):
    depth = w_in.shape[0]
    bp, seq, _ = x_prompt.shape
    bs, lq, _ = x_sample.shape
    slopes = jnp.exp2(-8.0 * jnp.arange(1, ATT_HEADS + 1, dtype=F32) / ATT_HEADS)
    head_of = jnp.arange(D_INNER, dtype=jnp.int32) // SSD_HEADDIM
    e01 = (jnp.arange(LANES, dtype=jnp.int32)[:, None] == head_of[None, :]).astype(BF16)
    et01 = e01.T
    h0_all = state_ssm.reshape(depth, bs, D_INNER, D_STATE)

    o_z, o_xbc, o_dt = 0, D_INNER, D_INNER + CONV_DIM
    o_q = o_dt + SSD_HEADS
    o_k, o_v, o_ga, o_gb = o_q + ATT_WIDTH, o_q + 2 * ATT_WIDTH, o_q + 3 * ATT_WIDTH, o_q + 4 * ATT_WIDTH

    xp = x_prompt.reshape(bp * seq, D_MODEL)
    xs = x_sample.reshape(bs * lq, D_MODEL)
    outs = {k: [] for k in ("hp", "cp", "ks", "vs", "hs", "cs")}
    pps = []
    row = lambda v: v.astype(F32).reshape(1, -1)
    heads_rows = lambda a: a.reshape(bs, lq * ATT_HEADS, ATT_HEAD_DIM)
    for l in range(depth):
        wl = w_in[l]
        w_main = jnp.concatenate(
            [wl[:, o_z:o_xbc], wl[:, o_ga:o_gb], wl[:, o_xbc:o_dt], wl[:, o_q:o_ga], wl[:, o_gb:]],
            axis=1).astype(BF16)
        w_dt = jnp.pad(wl[:, o_dt:o_q], ((0, 0), (0, LANES - SSD_HEADS)))
        w_dt_hi = w_dt.astype(BF16)
        w_dt_lo = (w_dt - w_dt_hi.astype(F32)).astype(BF16)
        w1i, w1o = w_ffn1_in[l].astype(BF16), w_ffn1_out[l].astype(BF16)
        w2i, w2o = w_ffn2_in[l].astype(BF16), w_ffn2_out[l].astype(BF16)
        wpa, wpb, wo = w_proj_a[l].astype(BF16), w_proj_b[l].astype(BF16), w_out[l].astype(BF16)
        consts = (conv_w[l], row(conv_b[l]), _pad_lanes(dt_bias[l]), _pad_lanes(a_log[l]),
                  row(jnp.repeat(d_skip[l], SSD_HEADDIM)), row(ssd_norm[l]), e01, et01)
        g1, gm, g2 = row(ln_ffn1[l]), row(ln_mix[l]), row(ln_ffn2[l])

        xp = _ffn(xp, g1, w1i, w1o)
        xs = _ffn(xs, g1, w1i, w1o)
        pp, dtp = _inproj(xp, gm, w_main, w_dt_hi, w_dt_lo)
        ps, dts = _inproj(xs, gm, w_main, w_dt_hi, w_dt_lo)
        pp3 = pp.reshape(bp, seq, P_COLS)
        ps3 = ps.reshape(bs, lq, P_COLS)
        pps.append(pp)
        ks_new = ps3[:, :, COL_K:COL_K + ATT_WIDTH].reshape(bs, lq, ATT_HEADS, ATT_HEAD_DIM)
        vs_new = ps3[:, :, COL_V:COL_V + ATT_WIDTH].reshape(bs, lq, ATT_HEADS, ATT_HEAD_DIM)

        ya_p, h_p = _ssd(pp3, dtp.reshape(bp, seq, LANES), consts)
        ya_s, h_s = _ssd(ps3, dts.reshape(bs, lq, LANES), consts, state=(h0_all, state_conv, l))
        yb_p = _moba_p(pp3, slopes)
        yb_s = _moba_s(heads_rows(ps3[:, :, COL_Q:COL_Q + ATT_WIDTH]), heads_rows(ks_new), heads_rows(vs_new),
                       cache_k, cache_v, page_table, l)

        xp = _merge(xp, ya_p.reshape(bp * seq, D_INNER), yb_p.reshape(bp * seq, ATT_WIDTH), pp, wpa, wpb, wo)
        xs = _merge(xs, ya_s.reshape(bs * lq, D_INNER), yb_s.reshape(bs * lq, ATT_WIDTH), ps, wpa, wpb, wo)
        g_last = row(ln_final) if l == depth - 1 else None
        xp = _ffn(xp, g2, w2i, w2o, g_last)
        xs = _ffn(xs, g2, w2i, w2o, g_last)

        outs["hp"].append(h_p.reshape(bp, SSD_HEADS, SSD_HEADDIM, D_STATE))
        outs["cp"].append(pp3[:, seq - (CONV_W - 1):, COL_XBC:COL_XBC + CONV_DIM])
        outs["ks"].append(ks_new)
        outs["vs"].append(vs_new)
        outs["hs"].append(h_s.reshape(bs, SSD_HEADS, SSD_HEADDIM, D_STATE))
        if lq >= CONV_W - 1:
            cs = ps3[:, lq - (CONV_W - 1):, COL_XBC:COL_XBC + CONV_DIM]
        else:
            cs = jnp.concatenate([state_conv[l], ps3[:, :, COL_XBC:COL_XBC + CONV_DIM]], axis=1)[:, -(CONV_W - 1):]
        outs["cs"].append(cs)

    y_prompt = xp.reshape(bp, seq, D_MODEL)
    y_sample = xs.reshape(bs, lq, D_MODEL)
    k_prompt, v_prompt = _kvout(pps)
    kv5 = lambda a: a.reshape(depth, bp, seq, ATT_HEADS, ATT_HEAD_DIM)
    st = lambda k: jnp.stack(outs[k])
    return (y_prompt, y_sample, kv5(k_prompt), kv5(v_prompt), st("hp"), st("cp"), st("ks"), st("vs"), st("hs"),
            st("cs"))
```

```python
import functools

import jax
import jax.numpy as jnp
from jax import lax
from jax.experimental import pallas as pl
from jax.experimental.pallas import tpu as pltpu

F32 = jnp.float32
BF16 = jnp.bfloat16

D_MODEL = 1024
D_INNER = 2048
SSD_HEADDIM = 64
SSD_HEADS = 32
SSD_GROUPS = 4
GROUP_W = D_INNER // SSD_GROUPS
D_STATE = 128
CONV_W = 4
CONV_DIM = 3072
SSD_CHUNK = 128
ATT_HEADS = 8
ATT_HEAD_DIM = 128
ATT_WIDTH = 1024
MOBA_BLOCK = 256
MOBA_TOPK = 3
PAGE_SIZE = 128
D_FF = 2816
FFN_SCALE = 0.5
EPS = 1e-6

LANES = 128
SUBLANES = 8
VMEM_LIMIT = 48 * 1024 * 1024

P_COLS = 8192
COL_Z_S = 8192
COL_XBC = 0
COL_Q = 3072
COL_K = 4096
COL_V = 5120
COL_GA = 6144
COL_GB = 7168

_NT = (((1,), (1,)), ((), ()))
_TN = (((0,), (0,)), ((), ()))


def _params(sem):
    return pltpu.CompilerParams(dimension_semantics=sem, vmem_limit_bytes=VMEM_LIMIT)


def _rms(x, g):
    return x * lax.rsqrt(jnp.mean(x * x, axis=-1, keepdims=True) + EPS) * g


def _silu(x):
    h = 0.5 * x
    return h * jnp.tanh(h) + h


def _split3(x):
    x1 = x.astype(BF16)
    r = x - x1.astype(F32)
    x2 = r.astype(BF16)
    r = r - x2.astype(F32)
    return x1, x2, r.astype(BF16)


def _sel_left(a01, x, pieces=3):
    return sum(jnp.dot(a01, p, preferred_element_type=F32) for p in _split3(x)[:pieces])


def _sel_right(x, e01, pieces=3):
    return sum(jnp.dot(p, e01, preferred_element_type=F32) for p in _split3(x)[:pieces])


def _ffn_kernel(x_ref, g_ref, wa_ref, wb_ref, wo_ref, gout_ref, o_ref, h_scr, acc_scr, *, nf, norm_out):
    f = pl.program_id(1)

    @pl.when(f == 0)
    def _():
        h_scr[...] = _rms(x_ref[...], g_ref[...]).astype(BF16)
        acc_scr[...] = jnp.zeros_like(acc_scr)

    h = h_scr[...]
    a = jnp.dot(h, wa_ref[...], preferred_element_type=F32)
    b = jnp.dot(h, wb_ref[...], preferred_element_type=F32)
    acc_scr[...] += jnp.dot((_silu(a) * b).astype(BF16), wo_ref[...], preferred_element_type=F32)

    @pl.when(f == nf - 1)
    def _():
        y = x_ref[...] + FFN_SCALE * acc_scr[...]
        o_ref[...] = _rms(y, gout_ref[...]) if norm_out else y


def _ffn(x, g, w_in, w_out, g_out=None):
    t = x.shape[0]
    tm = min(512, t)
    nf = 2
    tf = D_FF // nf
    norm_out = g_out is not None
    return pl.pallas_call(
        functools.partial(_ffn_kernel, nf=nf, norm_out=norm_out),
        grid=(t // tm, nf),
        in_specs=[
            pl.BlockSpec((tm, D_MODEL), lambda i, f: (i, 0)),
            pl.BlockSpec((1, D_MODEL), lambda i, f: (0, 0)),
            pl.BlockSpec((D_MODEL, tf), lambda i, f: (0, f)),
            pl.BlockSpec((D_MODEL, tf), lambda i, f: (0, f + nf)),
            pl.BlockSpec((tf, D_MODEL), lambda i, f: (f, 0)),
            pl.BlockSpec((1, D_MODEL), lambda i, f: (0, 0)),
        ],
        out_specs=pl.BlockSpec((tm, D_MODEL), lambda i, f: (i, 0)),
        out_shape=jax.ShapeDtypeStruct((t, D_MODEL), F32),
        scratch_shapes=[pltpu.VMEM((tm, D_MODEL), BF16), pltpu.VMEM((tm, D_MODEL), F32)],
        compiler_params=_params(("parallel", "arbitrary")),
        name="ffn",
    )(x, g, w_in, w_in, w_out, g_out if norm_out else g)


def _inproj_kernel(x_ref, g_ref, w_ref, wdth_ref, wdtl_ref, p_ref, dt_ref, h_scr):
    n = pl.program_id(1)

    @pl.when(n == 0)
    def _():
        h = _rms(x_ref[...], g_ref[...])
        h_hi = h.astype(BF16)
        h_lo = (h - h_hi.astype(F32)).astype(BF16)
        h_scr[...] = h_hi
        dt_ref[...] = (jnp.dot(h_hi, wdth_ref[...], preferred_element_type=F32)
                       + jnp.dot(h_hi, wdtl_ref[...], preferred_element_type=F32)
                       + jnp.dot(h_lo, wdth_ref[...], preferred_element_type=F32))

    p_ref[...] = jnp.dot(h_scr[...], w_ref[...], preferred_element_type=F32)


def _inproj(x, g, w_main, w_dt_hi, w_dt_lo):
    t = x.shape[0]
    tm = 1024 if t % 1024 == 0 else min(512, t)
    tn = 1024
    cols = w_main.shape[1]
    return pl.pallas_call(
        _inproj_kernel,
        grid=(t // tm, cols // tn),
        in_specs=[
            pl.BlockSpec((tm, D_MODEL), lambda i, n: (i, 0)),
            pl.BlockSpec((1, D_MODEL), lambda i, n: (0, 0)),
            pl.BlockSpec((D_MODEL, tn), lambda i, n: (0, n)),
            pl.BlockSpec((D_MODEL, LANES), lambda i, n: (0, 0)),
            pl.BlockSpec((D_MODEL, LANES), lambda i, n: (0, 0)),
        ],
        out_specs=[
            pl.BlockSpec((tm, tn), lambda i, n: (i, n)),
            pl.BlockSpec((tm, LANES), lambda i, n: (i, 0)),
        ],
        out_shape=[jax.ShapeDtypeStruct((t, cols), F32), jax.ShapeDtypeStruct((t, LANES), F32)],
        scratch_shapes=[pltpu.VMEM((tm, D_MODEL), BF16)],
        compiler_params=_params(("parallel", "arbitrary")),
        name="inproj",
    )(x, g, w_main, w_dt_hi, w_dt_lo)


_HIST = SUBLANES - (CONV_W - 1)


def _ssd_kernel(*refs, q_real, q_pad, nc, has_init):
    if has_init:
        (xbc_ref, z_ref, dt_ref, h0_ref, cbuf_ref, cw_ref, cb_ref, dtb_ref, alog_ref, dsk_ref,
         ng_ref, e_ref, y_ref, hout_ref, ext_scr, h_scr, dt_scr, z_scr) = refs
    else:
        (xbc_ref, x_ref, dt_ref, cw_ref, cb_ref, dtb_ref, alog_ref, dsk_ref,
         ng_ref, e_ref, gm_ref, wz_ref, y_ref, hout_ref, ext_scr, h_scr, dt_scr) = refs
    c = pl.program_id(1)

    @pl.when(c == 0)
    def _():
        ext_scr[...] = jnp.zeros_like(ext_scr)
        dt_scr[...] = jnp.zeros_like(dt_scr)
        if has_init:
            z_scr[...] = jnp.zeros_like(z_scr)
        if has_init:
            h_scr[...] = h0_ref[0, 0]
            ext_scr[_HIST:SUBLANES, :] = cbuf_ref[0, 0]
        else:
            h_scr[...] = jnp.zeros_like(h_scr)

    ext_scr[SUBLANES:SUBLANES + q_real, :] = xbc_ref[0]
    acc = cb_ref[...] + cw_ref[0:1, :] * ext_scr[_HIST:_HIST + q_pad, :]
    for w in range(1, CONV_W):
        acc = acc + cw_ref[w:w + 1, :] * ext_scr[_HIST + w:_HIST + w + q_pad, :]
    if nc > 1:
        ext_scr[_HIST:SUBLANES, :] = ext_scr[SUBLANES + q_real - (CONV_W - 1):SUBLANES + q_real, :]
    u = _silu(acc)
    xs = u[:, :D_INNER]
    b_bf = u[:, D_INNER:D_INNER + SSD_GROUPS * D_STATE].astype(BF16)
    c_bf = u[:, D_INNER + SSD_GROUPS * D_STATE:].astype(BF16)

    dt_scr[0:q_real, :] = dt_ref[0]
    if has_init:
        z_scr[0:q_real, :] = z_ref[0]
        z = z_scr[...]
    else:
        z = jnp.dot(_rms(x_ref[0], gm_ref[...]).astype(BF16), wz_ref[...], preferred_element_type=F32)
    rows =lax.broadcasted_iota(jnp.int32, (q_pad, LANES), 0)
    dtr = dt_scr[...] + dtb_ref[...]
    softplus = jnp.maximum(dtr, 0.0) + jnp.log1p(jnp.exp(-jnp.abs(dtr)))
    dt = jnp.where(rows < q_real, softplus, 0.0)
    da = dt * (-jnp.exp(alog_ref[...]))

    ri = lax.broadcasted_iota(jnp.int32, (q_pad, q_pad), 0)
    ci = lax.broadcasted_iota(jnp.int32, (q_pad, q_pad), 1)
    causal = ri >= ci
    a_cum = _sel_left(causal.astype(BF16), da)
    a_last = a_cum[q_pad - 1:q_pad, :]
    if q_pad < LANES:
        a_sq = jnp.concatenate([a_cum, jnp.zeros((LANES - q_pad, LANES), F32)], axis=0)
    else:
        a_sq = a_cum
    a_cum_t = a_sq.T

    e01 = e_ref[...]
    dt_x = _sel_right(dt, e01, pieces=2)
    eac_x = _sel_right(jnp.exp(a_cum), e01, pieces=2)
    dte_x = _sel_right(jnp.exp(a_last - a_cum), e01, pieces=2)
    xw = xs * dt_x
    xw_bf = xw.astype(BF16)
    xwd_bf = (xw * dte_x).astype(BF16)

    dec_rows = jnp.broadcast_to(jnp.exp(a_cum_t[:, q_pad - 1:q_pad]), (LANES, LANES))

    lane = lax.broadcasted_iota(jnp.int32, (q_pad, LANES), 1)
    ys = []
    for g in range(SSD_GROUPS):
        cg = c_bf[:, g * D_STATE:(g + 1) * D_STATE]
        bg = b_bf[:, g * D_STATE:(g + 1) * D_STATE]
        cb = lax.dot_general(cg, bg, _NT, preferred_element_type=F32)
        hg = h_scr[g * GROUP_W:(g + 1) * GROUP_W, :]
        y_off = lax.dot_general(cg, hg.astype(BF16), _NT, preferred_element_type=F32)
        parts = []
        for k in range(4):
            slab = g * 4 + k
            xwp = xw_bf[:, slab * LANES:(slab + 1) * LANES]
            pair = []
            for h in (2 * slab, 2 * slab + 1):
                seg = a_cum[:, h:h + 1] - a_cum_t[h:h + 1, :q_pad]
                decay_in = jnp.exp(jnp.where(causal, seg, -jnp.inf))
                pair.append(jnp.dot((cb * decay_in).astype(BF16), xwp, preferred_element_type=F32))
            parts.append(jnp.where(lane < SSD_HEADDIM, pair[0], pair[1]))
        y_diag = jnp.concatenate(parts, axis=1)
        ys.append(y_diag + y_off * eac_x[:, g * GROUP_W:(g + 1) * GROUP_W])
        s_new = lax.dot_general(xwd_bf[:, g * GROUP_W:(g + 1) * GROUP_W], bg, _TN,
                                preferred_element_type=F32)
        for hh in range(GROUP_W // SSD_HEADDIM):
            head = g * (GROUP_W // SSD_HEADDIM) + hh
            r0, r1 = hh * SSD_HEADDIM, (hh + 1) * SSD_HEADDIM
            h_scr[g * GROUP_W + r0:g * GROUP_W + r1, :] = (hg[r0:r1, :] * dec_rows[head:head + 1, :]
                                                          + s_new[r0:r1, :])

    y = jnp.concatenate(ys, axis=1) + dsk_ref[...] * xs
    gated = y * _silu(z)
    outs = []
    for g in range(SSD_GROUPS):
        ug = gated[:, g * GROUP_W:(g + 1) * GROUP_W]
        outs.append(ug * lax.rsqrt(jnp.mean(ug * ug, axis=-1, keepdims=True) + EPS))
    out = jnp.concatenate(outs, axis=1) * ng_ref[...]
    y_ref[0] = out[0:q_real, :]

    @pl.when(c == nc - 1)
    def _():
        hout_ref[0] = h_scr[...]


def _ssd(p3, dt3, consts, x3=None, gate_w=None, state=None):
    b, L, _ = p3.shape
    q_real = min(SSD_CHUNK, L)
    q_pad = -(-q_real // SUBLANES) * SUBLANES
    nc = L // q_real
    has_init = state is not None
    assert has_init or q_real == q_pad
    cw, cbias, dtb, alog, dsk, ng, e01 = consts

    def const(shape):
        return pl.BlockSpec(shape, lambda i, c: (0,) * len(shape))

    xbc_spec = pl.BlockSpec((1, q_real, CONV_DIM), lambda i, c: (i, c, COL_XBC // CONV_DIM))
    dt_spec = pl.BlockSpec((1, q_real, LANES), lambda i, c: (i, c, 0))
    shared = [const((CONV_W, CONV_DIM)), const((1, CONV_DIM)), const((1, LANES)), const((1, LANES)),
              const((1, D_INNER)), const((1, D_INNER)), const((LANES, D_INNER))]
    scratch = [pltpu.VMEM((SUBLANES + q_pad, CONV_DIM), F32), pltpu.VMEM((D_INNER, D_STATE), F32),
               pltpu.VMEM((q_pad, LANES), F32)]
    if has_init:
        h0, cbuf, layer = state
        in_specs = [xbc_spec, pl.BlockSpec((1, q_real, D_INNER), lambda i, c: (i, c, COL_Z_S // D_INNER)), dt_spec,
                    pl.BlockSpec((1, 1, D_INNER, D_STATE), lambda i, c: (layer, i, 0, 0)),
                    pl.BlockSpec((1, 1, CONV_W - 1, CONV_DIM), lambda i, c: (layer, i, 0, 0))] + shared
        args = [p3, p3, dt3, h0, cbuf, cw, cbias, dtb, alog, dsk, ng, e01]
        scratch.append(pltpu.VMEM((q_pad, D_INNER), F32))
    else:
        gm, wz = gate_w
        in_specs = ([xbc_spec, pl.BlockSpec((1, q_real, D_MODEL), lambda i, c: (i, c, 0)), dt_spec] + shared
                    + [const((1, D_MODEL)), const((D_MODEL, D_INNER))])
        args = [p3, x3, dt3, cw, cbias, dtb, alog, dsk, ng, e01, gm, wz]
    return pl.pallas_call(
        functools.partial(_ssd_kernel, q_real=q_real, q_pad=q_pad, nc=nc, has_init=has_init),
        grid=(b, nc),
        in_specs=in_specs,
        out_specs=[
            pl.BlockSpec((1, q_real, D_INNER), lambda i, c: (i, c, 0)),
            pl.BlockSpec((1, D_INNER, D_STATE), lambda i, c: (i, 0, 0)),
        ],
        out_shape=[jax.ShapeDtypeStruct((b, L, D_INNER), F32),
                   jax.ShapeDtypeStruct((b, D_INNER, D_STATE), F32)],
        scratch_shapes=scratch,
        compiler_params=_params(("parallel", "arbitrary")),
        name="ssd_s" if has_init else "ssd_p",
    )(*args)


MOBA_P_HEADS = 4
GATE_ROWS = 2 * SUBLANES


def _moba_p_kernel(slopes_ref, q_ref, k_ref, v_ref, o_ref, kmh_scr, kml_scr, k_scr, vt_scr, bias_scr,
                   sel_scr, qs_scr, acc_scr, m_scr, l_scr, *, nb):
    hg = pl.program_id(1)
    j = pl.program_id(2)
    blk = MOBA_BLOCK
    G = MOBA_P_HEADS
    hd = ATT_HEAD_DIM

    @pl.when(j == 0)
    def _():
        krow = lax.broadcasted_iota(jnp.int32, (blk, blk), 0).astype(F32)
        for g in range(G):
            means = []
            for n in range(nb):
                kn = k_ref[0, n * blk:(n + 1) * blk, g * hd:(g + 1) * hd]
                means.append(jnp.sum(kn, axis=0, keepdims=True) * (1.0 / blk))
                k_scr[n, g] = kn.astype(BF16)
                vt_scr[n, g] = v_ref[0, n * blk:(n + 1) * blk, g * hd:(g + 1) * hd].T.astype(BF16)
            km = jnp.concatenate(means + [jnp.zeros((GATE_ROWS - nb, hd), F32)], axis=0)
            km_hi = km.astype(BF16)
            kmh_scr[g] = km_hi
            kml_scr[g] = (km - km_hi.astype(F32)).astype(BF16)
            bias_scr[g] = slopes_ref[hg * G + g] * krow

    rowid = lax.broadcasted_iota(jnp.int32, (SUBLANES, blk), 0)
    krow_i = lax.broadcasted_iota(jnp.int32, (blk, blk), 0)
    qcol_i = lax.broadcasted_iota(jnp.int32, (blk, blk), 1)
    causal = krow_i <= qcol_i
    scale = hd ** -0.5
    c_own = (j * blk).astype(F32)

    qs_all, gates, s_own = [], [], []
    for g in range(G):
        q = q_ref[0, :, g * hd:(g + 1) * hd]
        q_hi = q.astype(BF16)
        q_lo = (q - q_hi.astype(F32)).astype(BF16)
        qs = (q * scale).astype(BF16)
        qs_all.append(qs)
        gates.append((lax.dot_general(kmh_scr[g], q_hi, _NT, preferred_element_type=F32)
                      + lax.dot_general(kmh_scr[g], q_lo, _NT, preferred_element_type=F32)
                      + lax.dot_general(kml_scr[g], q_hi, _NT, preferred_element_type=F32))[0:SUBLANES, :])
        s_own.append(lax.dot_general(k_scr[j, g], qs, _NT, preferred_element_type=F32))
    sels, m0s, l0s, p_own = [], [], [], []
    for g in range(G):
        gate = jnp.where(rowid < j, gates[g], -jnp.inf)
        cnt = jnp.zeros((SUBLANES, blk), F32)
        for m in range(nb):
            gm = gate[m:m + 1, :]
            beats = (gm > gate) | ((gm == gate) & (m < rowid))
            cnt = cnt + jnp.where(beats, 1.0, 0.0)
        sels.append(jnp.where((cnt < MOBA_TOPK) & (rowid < j), 1.0, 0.0))
        s = jnp.where(causal, s_own[g] + bias_scr[g], -jnp.inf)
        smax = jnp.max(s, axis=0, keepdims=True)
        p = jnp.exp(s - smax)
        m0s.append(smax + slopes_ref[hg * G + g] * jnp.full((1, blk), c_own, F32))
        l0s.append(jnp.sum(p, axis=0, keepdims=True))
        p_own.append(p.astype(BF16))
    for g in range(G):
        acc_scr[g] = jnp.dot(vt_scr[j, g], p_own[g], preferred_element_type=F32)
        sel_scr[g], qs_scr[g], m_scr[g], l_scr[g] = sels[g], qs_all[g], m0s[g], l0s[g]

    def body(n, carry):
        c_n = jnp.full((1, blk), (n * blk).astype(F32), F32)
        scores = [lax.dot_general(k_scr[n, g], qs_scr[g], _NT, preferred_element_type=F32) for g in range(G)]
        stats, probs = [], []
        for g in range(G):
            c_blk = slopes_ref[hg * G + g] * c_n
            s = scores[g] + bias_scr[g]
            m_i = m_scr[g]
            m_use = jnp.maximum(m_i, jnp.max(s, axis=0, keepdims=True) + c_blk)
            p = jnp.exp(s - (m_use - c_blk))
            psum = jnp.sum(p, axis=0, keepdims=True)
            chosen = sel_scr[g, pl.ds(n, 1), :] > 0.0
            alpha = jnp.where(chosen, jnp.exp(m_i - m_use), 1.0)
            keep = jnp.where(chosen, 1.0, 0.0)
            stats.append((jnp.where(chosen, m_use, m_i), alpha * l_scr[g] + keep * psum, alpha, keep))
            probs.append(p.astype(BF16))
        for g in range(G):
            pv = jnp.dot(vt_scr[n, g], probs[g], preferred_element_type=F32)
            m_new, l_new, alpha, keep = stats[g]
            acc_scr[g] = alpha * acc_scr[g] + keep * pv
            m_scr[g], l_scr[g] = m_new, l_new
        return carry

    lax.fori_loop(0, j, body, 0)
    for g in range(G):
        o_ref[0, :, g * hd:(g + 1) * hd] = (acc_scr[g] / l_scr[g]).T


def _moba_p(p3, slopes):
    b, L, _ = p3.shape
    nb = L // MOBA_BLOCK
    G = MOBA_P_HEADS
    gw = G * ATT_HEAD_DIM
    assert nb <= SUBLANES and ATT_HEADS % G == 0 and COL_Q % gw == 0 and COL_K % gw == 0 and COL_V % gw == 0
    cq, ck, cv = COL_Q // gw, COL_K // gw, COL_V // gw
    return pl.pallas_call(
        functools.partial(_moba_p_kernel, nb=nb),
        grid=(b, ATT_HEADS // G, nb),
        in_specs=[
            pl.BlockSpec(memory_space=pltpu.SMEM),
            pl.BlockSpec((1, MOBA_BLOCK, gw), lambda i, h, j: (i, j, cq + h)),
            pl.BlockSpec((1, L, gw), lambda i, h, j: (i, 0, ck + h)),
            pl.BlockSpec((1, L, gw), lambda i, h, j: (i, 0, cv + h)),
        ],
        out_specs=pl.BlockSpec((1, MOBA_BLOCK, gw), lambda i, h, j: (i, j, h)),
        out_shape=jax.ShapeDtypeStruct((b, L, ATT_WIDTH), F32),
        scratch_shapes=[
            pltpu.VMEM((G, GATE_ROWS, ATT_HEAD_DIM), BF16),
            pltpu.VMEM((G, GATE_ROWS, ATT_HEAD_DIM), BF16),
            pltpu.VMEM((nb, G, MOBA_BLOCK, ATT_HEAD_DIM), BF16),
            pltpu.VMEM((nb, G, ATT_HEAD_DIM, MOBA_BLOCK), BF16),
            pltpu.VMEM((G, MOBA_BLOCK, MOBA_BLOCK), F32),
            pltpu.VMEM((G, SUBLANES, MOBA_BLOCK), F32),
            pltpu.VMEM((G, MOBA_BLOCK, ATT_HEAD_DIM), BF16),
            pltpu.VMEM((G, ATT_HEAD_DIM, MOBA_BLOCK), F32),
            pltpu.VMEM((G, 1, MOBA_BLOCK), F32),
            pltpu.VMEM((G, 1, MOBA_BLOCK), F32),
        ],
        compiler_params=_params(("parallel", "parallel", "arbitrary")),
        name="moba_p",
    )(slopes, p3, p3, p3)


MOBA_S_BLOCKS = 4


def _moba_s_kernel(pt_ref, q_ref, kn_ref, vn_ref, *rest, nb, lq, past_len):
    npg = 2 * MOBA_S_BLOCKS
    k_refs, v_refs = rest[:npg], rest[npg:2 * npg]
    o_ref, acc_scr, m_scr, l_scr, g_scr = rest[2 * npg:]
    step = pl.program_id(1)
    rows = lq * ATT_HEADS
    keys = PAGE_SIZE * ATT_HEADS
    rid = lax.broadcasted_iota(jnp.int32, (rows, 1), 0)
    head = rid % ATT_HEADS
    t_pos = (past_len + rid // ATT_HEADS).astype(F32)
    slope = jnp.exp2(-(head + 1).astype(F32) * (8.0 / ATT_HEADS))
    scale = ATT_HEAD_DIM ** -0.5

    q = q_ref[0]
    q_bf = (q * scale).astype(BF16)
    col = lax.broadcasted_iota(jnp.int32, (rows, keys), 1)
    in_page = jnp.where((col % ATT_HEADS) == head, slope * ((col // ATT_HEADS).astype(F32) - t_pos), -jnp.inf)

    raw, gates = [], []
    for i in range(MOBA_S_BLOCKS):
        k0 = k_refs[2 * i][0, 0]
        k1 = k_refs[2 * i + 1][0, 0]
        kmean = (jnp.sum(k0, axis=0) + jnp.sum(k1, axis=0)) * (1.0 / MOBA_BLOCK)
        gates.append(jnp.sum(q * jnp.concatenate([kmean] * lq, axis=0), axis=-1, keepdims=True))
        raw.append([lax.dot_general(q_bf, kh.reshape(keys, ATT_HEAD_DIM).astype(BF16), _NT,
                                    preferred_element_type=F32) for kh in (k0, k1)])
    stats, probs = [], []
    for i in range(MOBA_S_BLOCKS):
        n = step * MOBA_S_BLOCKS + i
        s = [raw[i][h] + in_page + slope * (n * MOBA_BLOCK + h * PAGE_SIZE).astype(F32) for h in range(2)]
        m_n = jnp.maximum(jnp.max(s[0], axis=-1, keepdims=True), jnp.max(s[1], axis=-1, keepdims=True))
        p = [jnp.exp(s[h] - m_n) for h in range(2)]
        l_n = jnp.sum(p[0], axis=-1, keepdims=True) + jnp.sum(p[1], axis=-1, keepdims=True)
        stats.append((m_n, l_n))
        probs.append([ph.astype(BF16) for ph in p])
    for i in range(MOBA_S_BLOCKS):
        n = step * MOBA_S_BLOCKS + i
        acc_scr[n] = sum(jnp.dot(probs[i][h], v_refs[2 * i + h][0, 0].reshape(keys, ATT_HEAD_DIM).astype(BF16),
                                 preferred_element_type=F32) for h in range(2))
        m_scr[n] = jnp.broadcast_to(stats[i][0], (rows, LANES))
        l_scr[n] = jnp.broadcast_to(stats[i][1], (rows, LANES))
        g_scr[n] = jnp.broadcast_to(gates[i], (rows, LANES))

    @pl.when(step == nb // MOBA_S_BLOCKS - 1)
    def _():
        qidx = rid // ATT_HEADS
        s_own = []
        for kk in range(lq):
            kn = jnp.concatenate([kn_ref[0, kk * ATT_HEADS:(kk + 1) * ATT_HEADS, :]] * lq, axis=0)
            s = jnp.sum(q * kn, axis=-1, keepdims=True) * scale
            s = s - slope * (t_pos - float(past_len + kk))
            s_own.append(jnp.where(kk <= qidx, s, -jnp.inf))
        gates = [g_scr[i][:, 0:1] for i in range(nb)]
        ms = [m_scr[i][:, 0:1] for i in range(nb)]
        sel = []
        for i in range(nb):
            cnt = jnp.zeros((rows, 1), F32)
            for m in range(nb):
                beats = (gates[m] > gates[i]) | ((gates[m] == gates[i]) & (m < i))
                cnt = cnt + jnp.where(beats, 1.0, 0.0)
            sel.append(cnt < MOBA_TOPK)
        m_tot = s_own[0]
        for kk in range(1, lq):
            m_tot = jnp.maximum(m_tot, s_own[kk])
        for i in range(nb):
            m_tot = jnp.maximum(m_tot, jnp.where(sel[i], ms[i], -jnp.inf))
        l_tot = jnp.zeros((rows, 1), F32)
        acc = jnp.zeros((rows, ATT_HEAD_DIM), F32)
        for kk in range(lq):
            w = jnp.exp(s_own[kk] - m_tot)
            l_tot = l_tot + w
            acc = acc + w * jnp.concatenate([vn_ref[0, kk * ATT_HEADS:(kk + 1) * ATT_HEADS, :]] * lq, axis=0)
        for i in range(nb):
            w = jnp.where(sel[i], jnp.exp(ms[i] - m_tot), 0.0)
            l_tot = l_tot + w * l_scr[i][:, 0:1]
            acc = acc + w * acc_scr[i]
        o_ref[0] = acc / l_tot


def _moba_s(q2, kn2, vn2, cache_k, cache_v, page_table, layer):
    b, rows, _ = q2.shape
    lq = rows // ATT_HEADS
    n_pages = page_table.shape[1]
    past_len = n_pages * PAGE_SIZE
    nb = past_len // MOBA_BLOCK
    ppb = MOBA_BLOCK // PAGE_SIZE
    assert ppb == 2 and nb >= MOBA_TOPK and lq <= MOBA_BLOCK and nb % MOBA_S_BLOCKS == 0
    npg = ppb * MOBA_S_BLOCKS

    def page(off):
        return pl.BlockSpec((1, 1, PAGE_SIZE, ATT_HEADS, ATT_HEAD_DIM),
                            lambda i, n, pt: (layer, pt[i, npg * n + off], 0, 0, 0))

    def new(_):
        return pl.BlockSpec((1, rows, ATT_HEAD_DIM), lambda i, n, pt: (i, 0, 0))

    pages = [page(off) for off in range(npg)]
    grid_spec = pltpu.PrefetchScalarGridSpec(
        num_scalar_prefetch=1,
        grid=(b, nb // MOBA_S_BLOCKS),
        in_specs=[new(0), new(1), new(2)] + pages + pages,
        out_specs=pl.BlockSpec((1, rows, ATT_HEAD_DIM), lambda i, n, pt: (i, 0, 0)),
        scratch_shapes=[
            pltpu.VMEM((nb, rows, ATT_HEAD_DIM), F32),
            pltpu.VMEM((nb, rows, LANES), F32),
            pltpu.VMEM((nb, rows, LANES), F32),
            pltpu.VMEM((nb, rows, LANES), F32),
        ],
    )
    return pl.pallas_call(
        functools.partial(_moba_s_kernel, nb=nb, lq=lq, past_len=past_len),
        grid_spec=grid_spec,
        out_shape=jax.ShapeDtypeStruct((b, rows, ATT_HEAD_DIM), F32),
        compiler_params=_params(("parallel", "arbitrary")),
        name="moba_s",
    )(page_table, q2, kn2, vn2, *([cache_k] * npg), *([cache_v] * npg))


def _merge_kernel(x_ref, ya_ref, yb_ref, ga_ref, gb_ref, wa_ref, wb_ref, wo_ref, o_ref):
    a = jnp.dot(ya_ref[...].astype(BF16), wa_ref[...], preferred_element_type=F32)
    b = jnp.dot(yb_ref[...].astype(BF16), wb_ref[...], preferred_element_type=F32)
    m = jax.nn.sigmoid(ga_ref[...]) * a + jax.nn.sigmoid(gb_ref[...]) * b
    o_ref[...] = x_ref[...] + jnp.dot(m.astype(BF16), wo_ref[...], preferred_element_type=F32)


def _merge(x, ya, yb, p, wpa, wpb, wo):
    t = x.shape[0]
    tm = min(512, t)
    return pl.pallas_call(
        _merge_kernel,
        grid=(t // tm,),
        in_specs=[
            pl.BlockSpec((tm, D_MODEL), lambda i: (i, 0)),
            pl.BlockSpec((tm, D_INNER), lambda i: (i, 0)),
            pl.BlockSpec((tm, ATT_WIDTH), lambda i: (i, 0)),
            pl.BlockSpec((tm, D_MODEL), lambda i: (i, COL_GA // D_MODEL)),
            pl.BlockSpec((tm, D_MODEL), lambda i: (i, COL_GB // D_MODEL)),
            pl.BlockSpec((D_INNER, D_MODEL), lambda i: (0, 0)),
            pl.BlockSpec((ATT_WIDTH, D_MODEL), lambda i: (0, 0)),
            pl.BlockSpec((D_MODEL, D_MODEL), lambda i: (0, 0)),
        ],
        out_specs=pl.BlockSpec((tm, D_MODEL), lambda i: (i, 0)),
        out_shape=jax.ShapeDtypeStruct((t, D_MODEL), F32),
        compiler_params=_params(("parallel",)),
        name="merge",
    )(x, ya, yb, p, p, wpa, wpb, wo)


def _kvout_kernel(*refs, depth):
    ok_ref, ov_ref = refs[2 * depth:]
    for l in range(depth):
        ok_ref[l] = refs[2 * l][...].reshape(ok_ref.shape[1:])
        ov_ref[l] = refs[2 * l + 1][...].reshape(ov_ref.shape[1:])


def _kvout(ps):
    depth = len(ps)
    t = ps[0].shape[0]
    tm = min(512, t)
    in_specs, args = [], []
    for p in ps:
        in_specs += [pl.BlockSpec((tm, ATT_WIDTH), lambda i: (i, COL_K // ATT_WIDTH)),
                     pl.BlockSpec((tm, ATT_WIDTH), lambda i: (i, COL_V // ATT_WIDTH))]
        args += [p, p]
    out_spec = pl.BlockSpec((depth, tm, ATT_HEADS, ATT_HEAD_DIM), lambda i: (0, i, 0, 0))
    out_shape = jax.ShapeDtypeStruct((depth, t, ATT_HEADS, ATT_HEAD_DIM), F32)
    return pl.pallas_call(
        functools.partial(_kvout_kernel, depth=depth),
        grid=(t // tm,),
        in_specs=in_specs,
        out_specs=[out_spec, out_spec],
        out_shape=[out_shape, out_shape],
        compiler_params=_params(("parallel",)),
        name="kvout",
    )(*args)


def _pad_lanes(v):
    return jnp.pad(v.astype(F32), (0, LANES - v.shape[0])).reshape(1, LANES)


def kernel(x_prompt, x_sample, cache_k, cache_v, state_ssm, state_conv, page_table, ln_ffn1, w_ffn1_in,
           w_ffn1_out, ln_mix, w_in, conv_w, conv_b, dt_bias, a_log, d_skip, ssd_norm, w_proj_a, w_proj_b,
           w_out, ln_ffn2, w_ffn2_in, w_ffn2_out, ln_final):
    depth = w_in.shape[0]
    bp, seq, _ = x_prompt.shape
    bs, lq, _ = x_sample.shape
    slopes = jnp.exp2(-8.0 * jnp.arange(1, ATT_HEADS + 1, dtype=F32) / ATT_HEADS)
    head_of = jnp.arange(D_INNER, dtype=jnp.int32) // SSD_HEADDIM
    e01 = (jnp.arange(LANES, dtype=jnp.int32)[:, None] == head_of[None, :]).astype(BF16)
    h0_all = state_ssm.reshape(depth, bs, D_INNER, D_STATE)

    o_z, o_xbc, o_dt = 0, D_INNER, D_INNER + CONV_DIM
    o_q = o_dt + SSD_HEADS

    xp = x_prompt.reshape(bp * seq, D_MODEL)
    xs = x_sample.reshape(bs * lq, D_MODEL)
    outs = {k: [] for k in ("hp", "cp", "ks", "vs", "hs", "cs")}
    pps = []
    row = lambda v: v.astype(F32).reshape(1, -1)
    heads_rows = lambda a: a.reshape(bs, lq * ATT_HEADS, ATT_HEAD_DIM)
    for l in range(depth):
        wl = w_in[l]
        w_main = jnp.concatenate([wl[:, o_xbc:o_dt], wl[:, o_q:]], axis=1).astype(BF16)
        w_z = wl[:, o_z:o_xbc].astype(BF16)
        w_dt = jnp.pad(wl[:, o_dt:o_q], ((0, 0), (0, LANES - SSD_HEADS)))
        w_dt_hi = w_dt.astype(BF16)
        w_dt_lo = (w_dt - w_dt_hi.astype(F32)).astype(BF16)
        w1i, w1o = w_ffn1_in[l].astype(BF16), w_ffn1_out[l].astype(BF16)
        w2i, w2o = w_ffn2_in[l].astype(BF16), w_ffn2_out[l].astype(BF16)
        wpa, wpb, wo = w_proj_a[l].astype(BF16), w_proj_b[l].astype(BF16), w_out[l].astype(BF16)
        consts = (conv_w[l], row(conv_b[l]), _pad_lanes(dt_bias[l]), _pad_lanes(a_log[l]),
                  row(jnp.repeat(d_skip[l], SSD_HEADDIM)), row(ssd_norm[l]), e01)
        g1, gm, g2 = row(ln_ffn1[l]), row(ln_mix[l]), row(ln_ffn2[l])

        xp = _ffn(xp, g1, w1i, w1o)
        xs = _ffn(xs, g1, w1i, w1o)
        pp, dtp = _inproj(xp, gm, w_main, w_dt_hi, w_dt_lo)
        ps, dts = _inproj(xs, gm, jnp.concatenate([w_main, w_z], axis=1), w_dt_hi, w_dt_lo)
        pp3 = pp.reshape(bp, seq, P_COLS)
        ps3 = ps.reshape(bs, lq, COL_Z_S + D_INNER)
        pps.append(pp)
        ks_new = ps3[:, :, COL_K:COL_K + ATT_WIDTH].reshape(bs, lq, ATT_HEADS, ATT_HEAD_DIM)
        vs_new = ps3[:, :, COL_V:COL_V + ATT_WIDTH].reshape(bs, lq, ATT_HEADS, ATT_HEAD_DIM)

        ya_p, h_p = _ssd(pp3, dtp.reshape(bp, seq, LANES), consts, x3=xp.reshape(bp, seq, D_MODEL),
                         gate_w=(gm, w_z))
        ya_s, h_s = _ssd(ps3, dts.reshape(bs, lq, LANES), consts, state=(h0_all, state_conv, l))
        yb_p = _moba_p(pp3, slopes)
        yb_s = _moba_s(heads_rows(ps3[:, :, COL_Q:COL_Q + ATT_WIDTH]), heads_rows(ks_new), heads_rows(vs_new),
                       cache_k, cache_v, page_table, l)

        xp = _merge(xp, ya_p.reshape(bp * seq, D_INNER), yb_p.reshape(bp * seq, ATT_WIDTH), pp, wpa, wpb, wo)
        xs = _merge(xs, ya_s.reshape(bs * lq, D_INNER), yb_s.reshape(bs * lq, ATT_WIDTH), ps, wpa, wpb, wo)
        g_last = row(ln_final) if l == depth - 1 else None
        xp = _ffn(xp, g2, w2i, w2o, g_last)
        xs = _ffn(xs, g2, w2i, w2o, g_last)

        outs["hp"].append(h_p.reshape(bp, SSD_HEADS, SSD_HEADDIM, D_STATE))
        outs["cp"].append(pp3[:, seq - (CONV_W - 1):, COL_XBC:COL_XBC + CONV_DIM])
        outs["ks"].append(ks_new)
        outs["vs"].append(vs_new)
        outs["hs"].append(h_s.reshape(bs, SSD_HEADS, SSD_HEADDIM, D_STATE))
        if lq >= CONV_W - 1:
            cs = ps3[:, lq - (CONV_W - 1):, COL_XBC:COL_XBC + CONV_DIM]
        else:
            cs = jnp.concatenate([state_conv[l], ps3[:, :, COL_XBC:COL_XBC + CONV_DIM]], axis=1)[:, -(CONV_W - 1):]
        outs["cs"].append(cs)

    y_prompt = xp.reshape(bp, seq, D_MODEL)
    y_sample = xs.reshape(bs, lq, D_MODEL)
    k_prompt, v_prompt = _kvout(pps)
    kv5 = lambda a: a.reshape(depth, bp, seq, ATT_HEADS, ATT_HEAD_DIM)
    st = lambda k: jnp.stack(outs[k])
    return (y_prompt, y_sample, kv5(k_prompt), kv5(v_prompt), st("hp"), st("cp"), st("ks"), st("vs"), st("hs"),
            st("cs"))
```

```python
import functools

import jax
import jax.numpy as jnp
from jax import lax
from jax.experimental import pallas as pl
from jax.experimental.pallas import tpu as pltpu

F32 = jnp.float32
BF16 = jnp.bfloat16

D_MODEL = 1024
D_INNER = 2048
SSD_HEADDIM = 64
SSD_HEADS = 32
SSD_GROUPS = 4
GROUP_W = D_INNER // SSD_GROUPS
D_STATE = 128
CONV_W = 4
CONV_DIM = 3072
SSD_CHUNK = 128
ATT_HEADS = 8
ATT_HEAD_DIM = 128
ATT_WIDTH = 1024
MOBA_BLOCK = 256
MOBA_TOPK = 3
PAGE_SIZE = 128
D_FF = 2816
FFN_SCALE = 0.5
EPS = 1e-6

LANES = 128
SUBLANES = 8
VMEM_LIMIT = 48 * 1024 * 1024

P_COLS = 8192
COL_Z_S = 8192
COL_XBC = 0
COL_Q = 3072
COL_K = 4096
COL_V = 5120
COL_GA = 6144
COL_GB = 7168

_NT = (((1,), (1,)), ((), ()))
_TN = (((0,), (0,)), ((), ()))


def _params(sem, vmem_limit=VMEM_LIMIT):
    return pltpu.CompilerParams(dimension_semantics=sem, vmem_limit_bytes=vmem_limit)


def _rms(x, g):
    return x * lax.rsqrt(jnp.mean(x * x, axis=-1, keepdims=True) + EPS) * g


def _silu(x):
    h = 0.5 * x
    return h * jnp.tanh(h) + h


def _split3(x):
    x1 = x.astype(BF16)
    r = x - x1.astype(F32)
    x2 = r.astype(BF16)
    r = r - x2.astype(F32)
    return x1, x2, r.astype(BF16)


def _sel_left(a01, x, pieces=3):
    return sum(jnp.dot(a01, p, preferred_element_type=F32) for p in _split3(x)[:pieces])


def _sel_right(x, e01, pieces=3):
    return sum(jnp.dot(p, e01, preferred_element_type=F32) for p in _split3(x)[:pieces])


def _ffn_kernel(x_ref, g_ref, wa_ref, wb_ref, wo_ref, gout_ref, o_ref, h_scr, acc_scr, *, nf, norm_out):
    f = pl.program_id(1)

    @pl.when(f == 0)
    def _():
        h_scr[...] = _rms(x_ref[...], g_ref[...]).astype(BF16)
        acc_scr[...] = jnp.zeros_like(acc_scr)

    h = h_scr[...]
    a = jnp.dot(h, wa_ref[...], preferred_element_type=F32)
    b = jnp.dot(h, wb_ref[...], preferred_element_type=F32)
    acc_scr[...] += jnp.dot((_silu(a) * b).astype(BF16), wo_ref[...], preferred_element_type=F32)

    @pl.when(f == nf - 1)
    def _():
        y = x_ref[...] + FFN_SCALE * acc_scr[...]
        o_ref[...] = _rms(y, gout_ref[...]) if norm_out else y


def _ffn(x, g, w_in, w_out, g_out=None):
    t = x.shape[0]
    tm = min(512, t)
    nf = 2
    tf = D_FF // nf
    norm_out = g_out is not None
    return pl.pallas_call(
        functools.partial(_ffn_kernel, nf=nf, norm_out=norm_out),
        grid=(t // tm, nf),
        in_specs=[
            pl.BlockSpec((tm, D_MODEL), lambda i, f: (i, 0)),
            pl.BlockSpec((1, D_MODEL), lambda i, f: (0, 0)),
            pl.BlockSpec((D_MODEL, tf), lambda i, f: (0, f)),
            pl.BlockSpec((D_MODEL, tf), lambda i, f: (0, f + nf)),
            pl.BlockSpec((tf, D_MODEL), lambda i, f: (f, 0)),
            pl.BlockSpec((1, D_MODEL), lambda i, f: (0, 0)),
        ],
        out_specs=pl.BlockSpec((tm, D_MODEL), lambda i, f: (i, 0)),
        out_shape=jax.ShapeDtypeStruct((t, D_MODEL), F32),
        scratch_shapes=[pltpu.VMEM((tm, D_MODEL), BF16), pltpu.VMEM((tm, D_MODEL), F32)],
        compiler_params=_params(("parallel", "arbitrary")),
        name="ffn",
    )(x, g, w_in, w_in, w_out, g_out if norm_out else g)


def _inproj_kernel(x_ref, g_ref, w_ref, wdth_ref, wdtl_ref, p_ref, dt_ref, h_scr):
    n = pl.program_id(1)

    @pl.when(n == 0)
    def _():
        h = _rms(x_ref[...], g_ref[...])
        h_hi = h.astype(BF16)
        h_lo = (h - h_hi.astype(F32)).astype(BF16)
        h_scr[...] = h_hi
        dt_ref[...] = (jnp.dot(h_hi, wdth_ref[...], preferred_element_type=F32)
                       + jnp.dot(h_hi, wdtl_ref[...], preferred_element_type=F32)
                       + jnp.dot(h_lo, wdth_ref[...], preferred_element_type=F32))

    p_ref[...] = jnp.dot(h_scr[...], w_ref[...], preferred_element_type=F32)


def _inproj(x, g, w_main, w_dt_hi, w_dt_lo):
    t = x.shape[0]
    tm = 1024 if t % 1024 == 0 else min(512, t)
    tn = 1024
    cols = w_main.shape[1]
    return pl.pallas_call(
        _inproj_kernel,
        grid=(t // tm, cols // tn),
        in_specs=[
            pl.BlockSpec((tm, D_MODEL), lambda i, n: (i, 0)),
            pl.BlockSpec((1, D_MODEL), lambda i, n: (0, 0)),
            pl.BlockSpec((D_MODEL, tn), lambda i, n: (0, n)),
            pl.BlockSpec((D_MODEL, LANES), lambda i, n: (0, 0)),
            pl.BlockSpec((D_MODEL, LANES), lambda i, n: (0, 0)),
        ],
        out_specs=[
            pl.BlockSpec((tm, tn), lambda i, n: (i, n)),
            pl.BlockSpec((tm, LANES), lambda i, n: (i, 0)),
        ],
        out_shape=[jax.ShapeDtypeStruct((t, cols), F32), jax.ShapeDtypeStruct((t, LANES), F32)],
        scratch_shapes=[pltpu.VMEM((tm, D_MODEL), BF16)],
        compiler_params=_params(("parallel", "arbitrary")),
        name="inproj",
    )(x, g, w_main, w_dt_hi, w_dt_lo)


_HIST = SUBLANES - (CONV_W - 1)


def _ssd_kernel(*refs, q_real, q_pad, nc, has_init, n_prev):
    prev_refs = ()
    if has_init:
        (xbc_ref, z_ref, dt_ref, h0_ref, cbuf_ref, cw_ref, cb_ref, dtb_ref, alog_ref, dsk_ref,
         ng_ref, e_ref) = refs[:12]
        prev_refs = refs[12:12 + n_prev]
        y_ref, hout_ref, ext_scr, h_scr, dt_scr, z_scr = refs[12 + n_prev:]
    else:
        (xbc_ref, x_ref, dt_ref, cw_ref, cb_ref, dtb_ref, alog_ref, dsk_ref,
         ng_ref, e_ref, gm_ref, wz_ref, y_ref, hout_ref, ext_scr, h_scr, dt_scr) = refs
    c = pl.program_id(1)

    @pl.when(c == 0)
    def _():
        ext_scr[...] = jnp.zeros_like(ext_scr)
        dt_scr[...] = jnp.zeros_like(dt_scr)
        if has_init:
            z_scr[...] = jnp.zeros_like(z_scr)
        if has_init:
            h_scr[...] = h0_ref[0, 0]
            ext_scr[_HIST:SUBLANES, :] = cbuf_ref[0, 0]
        else:
            h_scr[...] = jnp.zeros_like(h_scr)

    ext_scr[SUBLANES:SUBLANES + q_real, :] = xbc_ref[0]
    acc = cb_ref[...] + cw_ref[0:1, :] * ext_scr[_HIST:_HIST + q_pad, :]
    for w in range(1, CONV_W):
        acc = acc + cw_ref[w:w + 1, :] * ext_scr[_HIST + w:_HIST + w + q_pad, :]
    if nc > 1:
        ext_scr[_HIST:SUBLANES, :] = ext_scr[SUBLANES + q_real - (CONV_W - 1):SUBLANES + q_real, :]
    u = _silu(acc)
    xs = u[:, :D_INNER]
    b_bf = u[:, D_INNER:D_INNER + SSD_GROUPS * D_STATE].astype(BF16)
    c_bf = u[:, D_INNER + SSD_GROUPS * D_STATE:].astype(BF16)

    dt_scr[0:q_real, :] = dt_ref[0]
    if has_init:
        z_scr[0:q_real, :] = z_ref[0]
        z = z_scr[...]
    else:
        z = jnp.dot(_rms(x_ref[0], gm_ref[...]).astype(BF16), wz_ref[...], preferred_element_type=F32)
    rows =lax.broadcasted_iota(jnp.int32, (q_pad, LANES), 0)
    dtr = dt_scr[...] + dtb_ref[...]
    softplus = jnp.maximum(dtr, 0.0) + jnp.log1p(jnp.exp(-jnp.abs(dtr)))
    dt = jnp.where(rows < q_real, softplus, 0.0)
    da = dt * (-jnp.exp(alog_ref[...]))

    ri = lax.broadcasted_iota(jnp.int32, (q_pad, q_pad), 0)
    ci = lax.broadcasted_iota(jnp.int32, (q_pad, q_pad), 1)
    causal = ri >= ci
    a_cum = _sel_left(causal.astype(BF16), da)
    a_last = a_cum[q_pad - 1:q_pad, :]
    if q_pad < LANES:
        a_sq = jnp.concatenate([a_cum, jnp.zeros((LANES - q_pad, LANES), F32)], axis=0)
    else:
        a_sq = a_cum
    a_cum_t = a_sq.T

    e01 = e_ref[...]
    dt_x = _sel_right(dt, e01, pieces=2)
    eac_x = _sel_right(jnp.exp(a_cum), e01, pieces=2)
    dte_x = _sel_right(jnp.exp(a_last - a_cum), e01, pieces=2)
    xw = xs * dt_x
    xw_bf = xw.astype(BF16)
    xwd_bf = (xw * dte_x).astype(BF16)

    dec_rows = jnp.broadcast_to(jnp.exp(a_cum_t[:, q_pad - 1:q_pad]), (LANES, LANES))

    lane = lax.broadcasted_iota(jnp.int32, (q_pad, LANES), 1)
    ys = []
    for g in range(SSD_GROUPS):
        cg = c_bf[:, g * D_STATE:(g + 1) * D_STATE]
        bg = b_bf[:, g * D_STATE:(g + 1) * D_STATE]
        cb = lax.dot_general(cg, bg, _NT, preferred_element_type=F32)
        hg = h_scr[g * GROUP_W:(g + 1) * GROUP_W, :]
        y_off = lax.dot_general(cg, hg.astype(BF16), _NT, preferred_element_type=F32)
        parts = []
        for k in range(4):
            slab = g * 4 + k
            xwp = xw_bf[:, slab * LANES:(slab + 1) * LANES]
            pair = []
            for h in (2 * slab, 2 * slab + 1):
                seg = a_cum[:, h:h + 1] - a_cum_t[h:h + 1, :q_pad]
                decay_in = jnp.exp(jnp.where(causal, seg, -jnp.inf))
                pair.append(jnp.dot((cb * decay_in).astype(BF16), xwp, preferred_element_type=F32))
            parts.append(jnp.where(lane < SSD_HEADDIM, pair[0], pair[1]))
        y_diag = jnp.concatenate(parts, axis=1)
        ys.append(y_diag + y_off * eac_x[:, g * GROUP_W:(g + 1) * GROUP_W])
        s_new = lax.dot_general(xwd_bf[:, g * GROUP_W:(g + 1) * GROUP_W], bg, _TN,
                                preferred_element_type=F32)
        for hh in range(GROUP_W // SSD_HEADDIM):
            head = g * (GROUP_W // SSD_HEADDIM) + hh
            r0, r1 = hh * SSD_HEADDIM, (hh + 1) * SSD_HEADDIM
            h_scr[g * GROUP_W + r0:g * GROUP_W + r1, :] = (hg[r0:r1, :] * dec_rows[head:head + 1, :]
                                                          + s_new[r0:r1, :])

    y = jnp.concatenate(ys, axis=1) + dsk_ref[...] * xs
    gated = y * _silu(z)
    outs = []
    for g in range(SSD_GROUPS):
        ug = gated[:, g * GROUP_W:(g + 1) * GROUP_W]
        outs.append(ug * lax.rsqrt(jnp.mean(ug * ug, axis=-1, keepdims=True) + EPS))
    out = jnp.concatenate(outs, axis=1) * ng_ref[...]
    y_ref[0] = out[0:q_real, :]

    @pl.when(c == nc - 1)
    def _():
        for k in range(n_prev):
            hout_ref[k, 0] = prev_refs[k][0]
        hout_ref[n_prev, 0] = h_scr[...]


def _ssd(p3, dt3, consts, x3=None, gate_w=None, state=None, collect=()):
    b, L, _ = p3.shape
    q_real = min(SSD_CHUNK, L)
    q_pad = -(-q_real // SUBLANES) * SUBLANES
    nc = L // q_real
    has_init = state is not None
    assert has_init or q_real == q_pad
    cw, cbias, dtb, alog, dsk, ng, e01 = consts

    def const(shape):
        return pl.BlockSpec(shape, lambda i, c: (0,) * len(shape))

    xbc_spec = pl.BlockSpec((1, q_real, CONV_DIM), lambda i, c: (i, c, COL_XBC // CONV_DIM))
    dt_spec = pl.BlockSpec((1, q_real, LANES), lambda i, c: (i, c, 0))
    shared = [const((CONV_W, CONV_DIM)), const((1, CONV_DIM)), const((1, LANES)), const((1, LANES)),
              const((1, D_INNER)), const((1, D_INNER)), const((LANES, D_INNER))]
    scratch = [pltpu.VMEM((SUBLANES + q_pad, CONV_DIM), F32), pltpu.VMEM((D_INNER, D_STATE), F32),
               pltpu.VMEM((q_pad, LANES), F32)]
    if has_init:
        h0, cbuf, layer = state
        in_specs = [xbc_spec, pl.BlockSpec((1, q_real, D_INNER), lambda i, c: (i, c, COL_Z_S // D_INNER)), dt_spec,
                    pl.BlockSpec((1, 1, D_INNER, D_STATE), lambda i, c: (layer, i, 0, 0)),
                    pl.BlockSpec((1, 1, CONV_W - 1, CONV_DIM), lambda i, c: (layer, i, 0, 0))] + shared
        in_specs += [pl.BlockSpec((1, D_INNER, D_STATE), lambda i, c: (i, 0, 0)) for _ in collect]
        args = [p3, p3, dt3, h0, cbuf, cw, cbias, dtb, alog, dsk, ng, e01, *collect]
        scratch.append(pltpu.VMEM((q_pad, D_INNER), F32))
    else:
        assert not collect
        gm, wz = gate_w
        in_specs = ([xbc_spec, pl.BlockSpec((1, q_real, D_MODEL), lambda i, c: (i, c, 0)), dt_spec] + shared
                    + [const((1, D_MODEL)), const((D_MODEL, D_INNER))])
        args = [p3, x3, dt3, cw, cbias, dtb, alog, dsk, ng, e01, gm, wz]
    n_out = len(collect) + 1
    return pl.pallas_call(
        functools.partial(_ssd_kernel, q_real=q_real, q_pad=q_pad, nc=nc, has_init=has_init,
                          n_prev=len(collect)),
        grid=(b, nc),
        in_specs=in_specs,
        out_specs=[
            pl.BlockSpec((1, q_real, D_INNER), lambda i, c: (i, c, 0)),
            pl.BlockSpec((n_out, 1, D_INNER, D_STATE), lambda i, c: (0, i, 0, 0)),
        ],
        out_shape=[jax.ShapeDtypeStruct((b, L, D_INNER), F32),
                   jax.ShapeDtypeStruct((n_out, b, D_INNER, D_STATE), F32)],
        scratch_shapes=scratch,
        compiler_params=_params(("parallel", "arbitrary")),
        name="ssd_s" if has_init else "ssd_p",
    )(*args)


MOBA_P_HEADS = 8
MOBA_P_VMEM = 58 * 1024 * 1024
GATE_ROWS = 2 * SUBLANES


def _moba_p_kernel(slopes_ref, q_ref, k_ref, v_ref, o_ref, kmh_scr, kml_scr, k_scr, vt_scr, bias_scr,
                   sel_scr, qs_scr, acc_scr, m_scr, l_scr, *, nb):
    hg = pl.program_id(1)
    j = pl.program_id(2)
    blk = MOBA_BLOCK
    G = MOBA_P_HEADS
    hd = ATT_HEAD_DIM

    @pl.when(j == 0)
    def _():
        krow = lax.broadcasted_iota(jnp.int32, (blk, blk), 0).astype(F32)
        for g in range(G):
            means = []
            for n in range(nb):
                kn = k_ref[0, n * blk:(n + 1) * blk, g * hd:(g + 1) * hd]
                means.append(jnp.sum(kn, axis=0, keepdims=True) * (1.0 / blk))
                k_scr[n, g] = kn.astype(BF16)
                vt_scr[n, g] = v_ref[0, n * blk:(n + 1) * blk, g * hd:(g + 1) * hd].T.astype(BF16)
            km = jnp.concatenate(means + [jnp.zeros((GATE_ROWS - nb, hd), F32)], axis=0)
            km_hi = km.astype(BF16)
            kmh_scr[g] = km_hi
            kml_scr[g] = (km - km_hi.astype(F32)).astype(BF16)
            bias_scr[g] = slopes_ref[hg * G + g] * krow

    rowid = lax.broadcasted_iota(jnp.int32, (SUBLANES, blk), 0)
    krow_i = lax.broadcasted_iota(jnp.int32, (blk, blk), 0)
    qcol_i = lax.broadcasted_iota(jnp.int32, (blk, blk), 1)
    causal = krow_i <= qcol_i
    scale = hd ** -0.5
    c_own = (j * blk).astype(F32)

    qs_all, gates, s_own = [], [], []
    for g in range(G):
        q = q_ref[0, :, g * hd:(g + 1) * hd]
        q_hi = q.astype(BF16)
        q_lo = (q - q_hi.astype(F32)).astype(BF16)
        qs = (q * scale).astype(BF16)
        qs_all.append(qs)
        gates.append((lax.dot_general(kmh_scr[g], q_hi, _NT, preferred_element_type=F32)
                      + lax.dot_general(kmh_scr[g], q_lo, _NT, preferred_element_type=F32)
                      + lax.dot_general(kml_scr[g], q_hi, _NT, preferred_element_type=F32))[0:SUBLANES, :])
        s_own.append(lax.dot_general(k_scr[j, g], qs, _NT, preferred_element_type=F32))
    sels, m0s, l0s, p_own = [], [], [], []
    for g in range(G):
        gate = jnp.where(rowid < j, gates[g], -jnp.inf)
        cnt = jnp.zeros((SUBLANES, blk), F32)
        for m in range(nb):
            gm = gate[m:m + 1, :]
            beats = (gm > gate) | ((gm == gate) & (m < rowid))
            cnt = cnt + jnp.where(beats, 1.0, 0.0)
        sels.append(jnp.where((cnt < MOBA_TOPK) & (rowid < j), 1.0, 0.0))
        s = jnp.where(causal, s_own[g] + bias_scr[g], -jnp.inf)
        smax = jnp.max(s, axis=0, keepdims=True)
        p = jnp.exp(s - smax)
        m0s.append(smax + slopes_ref[hg * G + g] * jnp.full((1, blk), c_own, F32))
        l0s.append(jnp.sum(p, axis=0, keepdims=True))
        p_own.append(p.astype(BF16))
    for g in range(G):
        acc_scr[g] = jnp.dot(vt_scr[j, g], p_own[g], preferred_element_type=F32)
        sel_scr[g], qs_scr[g], m_scr[g], l_scr[g] = sels[g], qs_all[g], m0s[g], l0s[g]

    def body(n, carry):
        c_n = jnp.full((1, blk), (n * blk).astype(F32), F32)
        scores = [lax.dot_general(k_scr[n, g], qs_scr[g], _NT, preferred_element_type=F32) for g in range(G)]
        stats, probs = [], []
        for g in range(G):
            c_blk = slopes_ref[hg * G + g] * c_n
            s = scores[g] + bias_scr[g]
            m_i = m_scr[g]
            m_use = jnp.maximum(m_i, jnp.max(s, axis=0, keepdims=True) + c_blk)
            p = jnp.exp(s - (m_use - c_blk))
            psum = jnp.sum(p, axis=0, keepdims=True)
            chosen = sel_scr[g, pl.ds(n, 1), :] > 0.0
            alpha = jnp.where(chosen, jnp.exp(m_i - m_use), 1.0)
            keep = jnp.where(chosen, 1.0, 0.0)
            stats.append((jnp.where(chosen, m_use, m_i), alpha * l_scr[g] + keep * psum, alpha, keep))
            probs.append(p.astype(BF16))
        for g in range(G):
            pv = jnp.dot(vt_scr[n, g], probs[g], preferred_element_type=F32)
            m_new, l_new, alpha, keep = stats[g]
            acc_scr[g] = alpha * acc_scr[g] + keep * pv
            m_scr[g], l_scr[g] = m_new, l_new
        return carry

    lax.fori_loop(0, j, body, 0)
    for g in range(G):
        o_ref[0, :, g * hd:(g + 1) * hd] = (acc_scr[g] / l_scr[g]).T


def _moba_p(p3, slopes):
    b, L, _ = p3.shape
    nb = L // MOBA_BLOCK
    G = MOBA_P_HEADS
    gw = G * ATT_HEAD_DIM
    assert nb <= SUBLANES and ATT_HEADS % G == 0 and COL_Q % gw == 0 and COL_K % gw == 0 and COL_V % gw == 0
    cq, ck, cv = COL_Q // gw, COL_K // gw, COL_V // gw
    return pl.pallas_call(
        functools.partial(_moba_p_kernel, nb=nb),
        grid=(b, ATT_HEADS // G, nb),
        in_specs=[
            pl.BlockSpec(memory_space=pltpu.SMEM),
            pl.BlockSpec((1, MOBA_BLOCK, gw), lambda i, h, j: (i, j, cq + h)),
            pl.BlockSpec((1, L, gw), lambda i, h, j: (i, 0, ck + h)),
            pl.BlockSpec((1, L, gw), lambda i, h, j: (i, 0, cv + h)),
        ],
        out_specs=pl.BlockSpec((1, MOBA_BLOCK, gw), lambda i, h, j: (i, j, h)),
        out_shape=jax.ShapeDtypeStruct((b, L, ATT_WIDTH), F32),
        scratch_shapes=[
            pltpu.VMEM((G, GATE_ROWS, ATT_HEAD_DIM), BF16),
            pltpu.VMEM((G, GATE_ROWS, ATT_HEAD_DIM), BF16),
            pltpu.VMEM((nb, G, MOBA_BLOCK, ATT_HEAD_DIM), BF16),
            pltpu.VMEM((nb, G, ATT_HEAD_DIM, MOBA_BLOCK), BF16),
            pltpu.VMEM((G, MOBA_BLOCK, MOBA_BLOCK), F32),
            pltpu.VMEM((G, SUBLANES, MOBA_BLOCK), F32),
            pltpu.VMEM((G, MOBA_BLOCK, ATT_HEAD_DIM), BF16),
            pltpu.VMEM((G, ATT_HEAD_DIM, MOBA_BLOCK), F32),
            pltpu.VMEM((G, 1, MOBA_BLOCK), F32),
            pltpu.VMEM((G, 1, MOBA_BLOCK), F32),
        ],
        compiler_params=_params(("parallel", "parallel", "arbitrary"), MOBA_P_VMEM),
        name="moba_p",
    )(slopes, p3, p3, p3)


MOBA_S_BLOCKS = 4


def _moba_s_kernel(pt_ref, q_ref, kn_ref, vn_ref, *rest, nb, lq, past_len):
    npg = 2 * MOBA_S_BLOCKS
    k_refs, v_refs = rest[:npg], rest[npg:2 * npg]
    o_ref, acc_scr, m_scr, l_scr, g_scr = rest[2 * npg:]
    step = pl.program_id(1)
    rows = lq * ATT_HEADS
    keys = PAGE_SIZE * ATT_HEADS
    rid = lax.broadcasted_iota(jnp.int32, (rows, 1), 0)
    head = rid % ATT_HEADS
    t_pos = (past_len + rid // ATT_HEADS).astype(F32)
    slope = jnp.exp2(-(head + 1).astype(F32) * (8.0 / ATT_HEADS))
    scale = ATT_HEAD_DIM ** -0.5

    q = q_ref[0]
    q_bf = (q * scale).astype(BF16)
    col = lax.broadcasted_iota(jnp.int32, (rows, keys), 1)
    in_page = jnp.where((col % ATT_HEADS) == head, slope * ((col // ATT_HEADS).astype(F32) - t_pos), -jnp.inf)

    raw, gates = [], []
    for i in range(MOBA_S_BLOCKS):
        k0 = k_refs[2 * i][0, 0]
        k1 = k_refs[2 * i + 1][0, 0]
        kmean = (jnp.sum(k0, axis=0) + jnp.sum(k1, axis=0)) * (1.0 / MOBA_BLOCK)
        gates.append(jnp.sum(q * jnp.concatenate([kmean] * lq, axis=0), axis=-1, keepdims=True))
        raw.append([lax.dot_general(q_bf, kh.reshape(keys, ATT_HEAD_DIM).astype(BF16), _NT,
                                    preferred_element_type=F32) for kh in (k0, k1)])
    stats, probs = [], []
    for i in range(MOBA_S_BLOCKS):
        n = step * MOBA_S_BLOCKS + i
        s = [raw[i][h] + in_page + slope * (n * MOBA_BLOCK + h * PAGE_SIZE).astype(F32) for h in range(2)]
        m_n = jnp.maximum(jnp.max(s[0], axis=-1, keepdims=True), jnp.max(s[1], axis=-1, keepdims=True))
        p = [jnp.exp(s[h] - m_n) for h in range(2)]
        l_n = jnp.sum(p[0], axis=-1, keepdims=True) + jnp.sum(p[1], axis=-1, keepdims=True)
        stats.append((m_n, l_n))
        probs.append([ph.astype(BF16) for ph in p])
    for i in range(MOBA_S_BLOCKS):
        n = step * MOBA_S_BLOCKS + i
        acc_scr[n] = sum(jnp.dot(probs[i][h], v_refs[2 * i + h][0, 0].reshape(keys, ATT_HEAD_DIM).astype(BF16),
                                 preferred_element_type=F32) for h in range(2))
        m_scr[n] = jnp.broadcast_to(stats[i][0], (rows, LANES))
        l_scr[n] = jnp.broadcast_to(stats[i][1], (rows, LANES))
        g_scr[n] = jnp.broadcast_to(gates[i], (rows, LANES))

    @pl.when(step == nb // MOBA_S_BLOCKS - 1)
    def _():
        qidx = rid // ATT_HEADS
        s_own = []
        for kk in range(lq):
            kn = jnp.concatenate([kn_ref[0, kk * ATT_HEADS:(kk + 1) * ATT_HEADS, :]] * lq, axis=0)
            s = jnp.sum(q * kn, axis=-1, keepdims=True) * scale
            s = s - slope * (t_pos - float(past_len + kk))
            s_own.append(jnp.where(kk <= qidx, s, -jnp.inf))
        gates = [g_scr[i][:, 0:1] for i in range(nb)]
        ms = [m_scr[i][:, 0:1] for i in range(nb)]
        sel = []
        for i in range(nb):
            cnt = jnp.zeros((rows, 1), F32)
            for m in range(nb):
                beats = (gates[m] > gates[i]) | ((gates[m] == gates[i]) & (m < i))
                cnt = cnt + jnp.where(beats, 1.0, 0.0)
            sel.append(cnt < MOBA_TOPK)
        m_tot = s_own[0]
        for kk in range(1, lq):
            m_tot = jnp.maximum(m_tot, s_own[kk])
        for i in range(nb):
            m_tot = jnp.maximum(m_tot, jnp.where(sel[i], ms[i], -jnp.inf))
        l_tot = jnp.zeros((rows, 1), F32)
        acc = jnp.zeros((rows, ATT_HEAD_DIM), F32)
        for kk in range(lq):
            w = jnp.exp(s_own[kk] - m_tot)
            l_tot = l_tot + w
            acc = acc + w * jnp.concatenate([vn_ref[0, kk * ATT_HEADS:(kk + 1) * ATT_HEADS, :]] * lq, axis=0)
        for i in range(nb):
            w = jnp.where(sel[i], jnp.exp(ms[i] - m_tot), 0.0)
            l_tot = l_tot + w * l_scr[i][:, 0:1]
            acc = acc + w * acc_scr[i]
        o_ref[0] = acc / l_tot


def _moba_s(q2, kn2, vn2, cache_k, cache_v, page_table, layer):
    b, rows, _ = q2.shape
    lq = rows // ATT_HEADS
    n_pages = page_table.shape[1]
    past_len = n_pages * PAGE_SIZE
    nb = past_len // MOBA_BLOCK
    ppb = MOBA_BLOCK // PAGE_SIZE
    assert ppb == 2 and nb >= MOBA_TOPK and lq <= MOBA_BLOCK and nb % MOBA_S_BLOCKS == 0
    npg = ppb * MOBA_S_BLOCKS

    def page(off):
        return pl.BlockSpec((1, 1, PAGE_SIZE, ATT_HEADS, ATT_HEAD_DIM),
                            lambda i, n, pt: (layer, pt[i, npg * n + off], 0, 0, 0))

    def new(_):
        return pl.BlockSpec((1, rows, ATT_HEAD_DIM), lambda i, n, pt: (i, 0, 0))

    pages = [page(off) for off in range(npg)]
    grid_spec = pltpu.PrefetchScalarGridSpec(
        num_scalar_prefetch=1,
        grid=(b, nb // MOBA_S_BLOCKS),
        in_specs=[new(0), new(1), new(2)] + pages + pages,
        out_specs=pl.BlockSpec((1, rows, ATT_HEAD_DIM), lambda i, n, pt: (i, 0, 0)),
        scratch_shapes=[
            pltpu.VMEM((nb, rows, ATT_HEAD_DIM), F32),
            pltpu.VMEM((nb, rows, LANES), F32),
            pltpu.VMEM((nb, rows, LANES), F32),
            pltpu.VMEM((nb, rows, LANES), F32),
        ],
    )
    return pl.pallas_call(
        functools.partial(_moba_s_kernel, nb=nb, lq=lq, past_len=past_len),
        grid_spec=grid_spec,
        out_shape=jax.ShapeDtypeStruct((b, rows, ATT_HEAD_DIM), F32),
        compiler_params=_params(("parallel", "arbitrary")),
        name="moba_s",
    )(page_table, q2, kn2, vn2, *([cache_k] * npg), *([cache_v] * npg))


def _merge_kernel(x_ref, ya_ref, yb_ref, ga_ref, gb_ref, wa_ref, wb_ref, wo_ref, o_ref):
    a = jnp.dot(ya_ref[...].astype(BF16), wa_ref[...], preferred_element_type=F32)
    b = jnp.dot(yb_ref[...].astype(BF16), wb_ref[...], preferred_element_type=F32)
    m = jax.nn.sigmoid(ga_ref[...]) * a + jax.nn.sigmoid(gb_ref[...]) * b
    o_ref[...] = x_ref[...] + jnp.dot(m.astype(BF16), wo_ref[...], preferred_element_type=F32)


def _merge(x, ya, yb, p, wpa, wpb, wo):
    t = x.shape[0]
    tm = min(512, t)
    return pl.pallas_call(
        _merge_kernel,
        grid=(t // tm,),
        in_specs=[
            pl.BlockSpec((tm, D_MODEL), lambda i: (i, 0)),
            pl.BlockSpec((tm, D_INNER), lambda i: (i, 0)),
            pl.BlockSpec((tm, ATT_WIDTH), lambda i: (i, 0)),
            pl.BlockSpec((tm, D_MODEL), lambda i: (i, COL_GA // D_MODEL)),
            pl.BlockSpec((tm, D_MODEL), lambda i: (i, COL_GB // D_MODEL)),
            pl.BlockSpec((D_INNER, D_MODEL), lambda i: (0, 0)),
            pl.BlockSpec((ATT_WIDTH, D_MODEL), lambda i: (0, 0)),
            pl.BlockSpec((D_MODEL, D_MODEL), lambda i: (0, 0)),
        ],
        out_specs=pl.BlockSpec((tm, D_MODEL), lambda i: (i, 0)),
        out_shape=jax.ShapeDtypeStruct((t, D_MODEL), F32),
        compiler_params=_params(("parallel",)),
        name="merge",
    )(x, ya, yb, p, p, wpa, wpb, wo)


def _kvout_kernel(*refs, depth):
    ok_ref, ov_ref = refs[2 * depth:]
    for l in range(depth):
        ok_ref[l] = refs[2 * l][...].reshape(ok_ref.shape[1:])
        ov_ref[l] = refs[2 * l + 1][...].reshape(ov_ref.shape[1:])


def _kvout(ps):
    depth = len(ps)
    t = ps[0].shape[0]
    tm = min(512, t)
    in_specs, args = [], []
    for p in ps:
        in_specs += [pl.BlockSpec((tm, ATT_WIDTH), lambda i: (i, COL_K // ATT_WIDTH)),
                     pl.BlockSpec((tm, ATT_WIDTH), lambda i: (i, COL_V // ATT_WIDTH))]
        args += [p, p]
    out_spec = pl.BlockSpec((depth, tm, ATT_HEADS, ATT_HEAD_DIM), lambda i: (0, i, 0, 0))
    out_shape = jax.ShapeDtypeStruct((depth, t, ATT_HEADS, ATT_HEAD_DIM), F32)
    return pl.pallas_call(
        functools.partial(_kvout_kernel, depth=depth),
        grid=(t // tm,),
        in_specs=in_specs,
        out_specs=[out_spec, out_spec],
        out_shape=[out_shape, out_shape],
        compiler_params=_params(("parallel",)),
        name="kvout",
    )(*args)


def _pad_lanes(v):
    return jnp.pad(v.astype(F32), (0, LANES - v.shape[0])).reshape(1, LANES)


def kernel(x_prompt, x_sample, cache_k, cache_v, state_ssm, state_conv, page_table, ln_ffn1, w_ffn1_in,
           w_ffn1_out, ln_mix, w_in, conv_w, conv_b, dt_bias, a_log, d_skip, ssd_norm, w_proj_a, w_proj_b,
           w_out, ln_ffn2, w_ffn2_in, w_ffn2_out, ln_final):
    depth = w_in.shape[0]
    bp, seq, _ = x_prompt.shape
    bs, lq, _ = x_sample.shape
    slopes = jnp.exp2(-8.0 * jnp.arange(1, ATT_HEADS + 1, dtype=F32) / ATT_HEADS)
    head_of = jnp.arange(D_INNER, dtype=jnp.int32) // SSD_HEADDIM
    e01 = (jnp.arange(LANES, dtype=jnp.int32)[:, None] == head_of[None, :]).astype(BF16)
    h0_all = state_ssm.reshape(depth, bs, D_INNER, D_STATE)

    o_z, o_xbc, o_dt = 0, D_INNER, D_INNER + CONV_DIM
    o_q = o_dt + SSD_HEADS

    xp = x_prompt.reshape(bp * seq, D_MODEL)
    xs = x_sample.reshape(bs * lq, D_MODEL)
    outs = {k: [] for k in ("hp", "cp", "ks", "vs", "hs", "cs")}
    pps = []
    row = lambda v: v.astype(F32).reshape(1, -1)
    heads_rows = lambda a: a.reshape(bs, lq * ATT_HEADS, ATT_HEAD_DIM)
    for l in range(depth):
        wl = w_in[l]
        w_main = jnp.concatenate([wl[:, o_xbc:o_dt], wl[:, o_q:]], axis=1).astype(BF16)
        w_z = wl[:, o_z:o_xbc].astype(BF16)
        w_dt = jnp.pad(wl[:, o_dt:o_q], ((0, 0), (0, LANES - SSD_HEADS)))
        w_dt_hi = w_dt.astype(BF16)
        w_dt_lo = (w_dt - w_dt_hi.astype(F32)).astype(BF16)
        w1i, w1o = w_ffn1_in[l].astype(BF16), w_ffn1_out[l].astype(BF16)
        w2i, w2o = w_ffn2_in[l].astype(BF16), w_ffn2_out[l].astype(BF16)
        wpa, wpb, wo = w_proj_a[l].astype(BF16), w_proj_b[l].astype(BF16), w_out[l].astype(BF16)
        consts = (conv_w[l], row(conv_b[l]), _pad_lanes(dt_bias[l]), _pad_lanes(a_log[l]),
                  row(jnp.repeat(d_skip[l], SSD_HEADDIM)), row(ssd_norm[l]), e01)
        g1, gm, g2 = row(ln_ffn1[l]), row(ln_mix[l]), row(ln_ffn2[l])

        xp = _ffn(xp, g1, w1i, w1o)
        xs = _ffn(xs, g1, w1i, w1o)
        pp, dtp = _inproj(xp, gm, w_main, w_dt_hi, w_dt_lo)
        ps, dts = _inproj(xs, gm, jnp.concatenate([w_main, w_z], axis=1), w_dt_hi, w_dt_lo)
        pp3 = pp.reshape(bp, seq, P_COLS)
        ps3 = ps.reshape(bs, lq, COL_Z_S + D_INNER)
        pps.append(pp)
        ks_new = ps3[:, :, COL_K:COL_K + ATT_WIDTH].reshape(bs, lq, ATT_HEADS, ATT_HEAD_DIM)
        vs_new = ps3[:, :, COL_V:COL_V + ATT_WIDTH].reshape(bs, lq, ATT_HEADS, ATT_HEAD_DIM)

        ya_p, h_p = _ssd(pp3, dtp.reshape(bp, seq, LANES), consts, x3=xp.reshape(bp, seq, D_MODEL),
                         gate_w=(gm, w_z))
        ya_s, h_s = _ssd(ps3, dts.reshape(bs, lq, LANES), consts, state=(h0_all, state_conv, l),
                         collect=tuple(outs["hs"]) if l == depth - 1 else ())
        yb_p = _moba_p(pp3, slopes)
        yb_s = _moba_s(heads_rows(ps3[:, :, COL_Q:COL_Q + ATT_WIDTH]), heads_rows(ks_new), heads_rows(vs_new),
                       cache_k, cache_v, page_table, l)

        xp = _merge(xp, ya_p.reshape(bp * seq, D_INNER), yb_p.reshape(bp * seq, ATT_WIDTH), pp, wpa, wpb, wo)
        xs = _merge(xs, ya_s.reshape(bs * lq, D_INNER), yb_s.reshape(bs * lq, ATT_WIDTH), ps, wpa, wpb, wo)
        g_last = row(ln_final) if l == depth - 1 else None
        xp = _ffn(xp, g2, w2i, w2o, g_last)
        xs = _ffn(xs, g2, w2i, w2o, g_last)

        outs["hp"].append(h_p.reshape(bp, SSD_HEADS, SSD_HEADDIM, D_STATE))
        outs["cp"].append(pp3[:, seq - (CONV_W - 1):, COL_XBC:COL_XBC + CONV_DIM])
        outs["ks"].append(ks_new)
        outs["vs"].append(vs_new)
        if l == depth - 1:
            ssm_sample = h_s.reshape(depth, bs, SSD_HEADS, SSD_HEADDIM, D_STATE)
        else:
            outs["hs"].append(h_s[0])
        if lq >= CONV_W - 1:
            cs = ps3[:, lq - (CONV_W - 1):, COL_XBC:COL_XBC + CONV_DIM]
        else:
            cs = jnp.concatenate([state_conv[l], ps3[:, :, COL_XBC:COL_XBC + CONV_DIM]], axis=1)[:, -(CONV_W - 1):]
        outs["cs"].append(cs)

    y_prompt = xp.reshape(bp, seq, D_MODEL)
    y_sample = xs.reshape(bs, lq, D_MODEL)
    k_prompt, v_prompt = _kvout(pps)
    kv5 = lambda a: a.reshape(depth, bp, seq, ATT_HEADS, ATT_HEAD_DIM)
    st = lambda k: jnp.stack(outs[k])
    return (y_prompt, y_sample, kv5(k_prompt), kv5(v_prompt), st("hp"), st("cp"), st("ks"), st("vs"), ssm_sample,
            st("cs"))
```

```python
import functools

import jax
import jax.numpy as jnp
from jax import lax
from jax.experimental import pallas as pl
from jax.experimental.pallas import tpu as pltpu

F32 = jnp.float32
BF16 = jnp.bfloat16

D_MODEL = 1024
D_INNER = 2048
SSD_HEADDIM = 64
SSD_HEADS = 32
SSD_GROUPS = 4
GROUP_W = D_INNER // SSD_GROUPS
D_STATE = 128
CONV_W = 4
CONV_DIM = 3072
SSD_CHUNK = 128
ATT_HEADS = 8
ATT_HEAD_DIM = 128
ATT_WIDTH = 1024
MOBA_BLOCK = 256
MOBA_TOPK = 3
PAGE_SIZE = 128
D_FF = 2816
FFN_SCALE = 0.5
EPS = 1e-6

LANES = 128
SUBLANES = 8
VMEM_LIMIT = 48 * 1024 * 1024

P_COLS = 8192
COL_Z_S = 8192
COL_XBC = 0
COL_Q = 3072
COL_K = 4096
COL_V = 5120
COL_GA = 6144
COL_GB = 7168

_NT = (((1,), (1,)), ((), ()))
_TN = (((0,), (0,)), ((), ()))


def _params(sem, vmem_limit=VMEM_LIMIT):
    return pltpu.CompilerParams(dimension_semantics=sem, vmem_limit_bytes=vmem_limit)


def _rms(x, g):
    return x * lax.rsqrt(jnp.mean(x * x, axis=-1, keepdims=True) + EPS) * g


def _silu(x):
    h = 0.5 * x
    return h * jnp.tanh(h) + h


def _split3(x):
    x1 = x.astype(BF16)
    r = x - x1.astype(F32)
    x2 = r.astype(BF16)
    r = r - x2.astype(F32)
    return x1, x2, r.astype(BF16)


def _sel_left(a01, x, pieces=3):
    return sum(jnp.dot(a01, p, preferred_element_type=F32) for p in _split3(x)[:pieces])


def _sel_right(x, e01, pieces=3):
    return sum(jnp.dot(p, e01, preferred_element_type=F32) for p in _split3(x)[:pieces])


def _ffn_kernel(x_ref, g_ref, wa_ref, wb_ref, wo_ref, gout_ref, o_ref, h_scr, acc_scr, *, nf, norm_out):
    f = pl.program_id(1)

    @pl.when(f == 0)
    def _():
        h_scr[...] = _rms(x_ref[...], g_ref[...]).astype(BF16)
        acc_scr[...] = jnp.zeros_like(acc_scr)

    h = h_scr[...]
    a = jnp.dot(h, wa_ref[...], preferred_element_type=F32)
    b = jnp.dot(h, wb_ref[...], preferred_element_type=F32)
    acc_scr[...] += jnp.dot((_silu(a) * b).astype(BF16), wo_ref[...], preferred_element_type=F32)

    @pl.when(f == nf - 1)
    def _():
        y = x_ref[...] + FFN_SCALE * acc_scr[...]
        o_ref[...] = _rms(y, gout_ref[...]) if norm_out else y


def _ffn(x, g, w_in, w_out, g_out=None):
    t = x.shape[0]
    tm = min(512, t)
    nf = 2
    tf = D_FF // nf
    norm_out = g_out is not None
    return pl.pallas_call(
        functools.partial(_ffn_kernel, nf=nf, norm_out=norm_out),
        grid=(t // tm, nf),
        in_specs=[
            pl.BlockSpec((tm, D_MODEL), lambda i, f: (i, 0)),
            pl.BlockSpec((1, D_MODEL), lambda i, f: (0, 0)),
            pl.BlockSpec((D_MODEL, tf), lambda i, f: (0, f)),
            pl.BlockSpec((D_MODEL, tf), lambda i, f: (0, f + nf)),
            pl.BlockSpec((tf, D_MODEL), lambda i, f: (f, 0)),
            pl.BlockSpec((1, D_MODEL), lambda i, f: (0, 0)),
        ],
        out_specs=pl.BlockSpec((tm, D_MODEL), lambda i, f: (i, 0)),
        out_shape=jax.ShapeDtypeStruct((t, D_MODEL), F32),
        scratch_shapes=[pltpu.VMEM((tm, D_MODEL), BF16), pltpu.VMEM((tm, D_MODEL), F32)],
        compiler_params=_params(("parallel", "arbitrary")),
        name="ffn",
    )(x, g, w_in, w_in, w_out, g_out if norm_out else g)


def _inproj_kernel(x_ref, g_ref, w_ref, wdth_ref, wdtl_ref, p_ref, dt_ref, h_scr):
    n = pl.program_id(1)

    @pl.when(n == 0)
    def _():
        h = _rms(x_ref[...], g_ref[...])
        h_hi = h.astype(BF16)
        h_lo = (h - h_hi.astype(F32)).astype(BF16)
        h_scr[...] = h_hi
        dt_ref[...] = (jnp.dot(h_hi, wdth_ref[...], preferred_element_type=F32)
                       + jnp.dot(h_hi, wdtl_ref[...], preferred_element_type=F32)
                       + jnp.dot(h_lo, wdth_ref[...], preferred_element_type=F32))

    p_ref[...] = jnp.dot(h_scr[...], w_ref[...], preferred_element_type=F32)


def _inproj(x, g, w_main, w_dt_hi, w_dt_lo):
    t = x.shape[0]
    tm = 1024 if t % 1024 == 0 else min(512, t)
    tn = 1024
    cols = w_main.shape[1]
    return pl.pallas_call(
        _inproj_kernel,
        grid=(t // tm, cols // tn),
        in_specs=[
            pl.BlockSpec((tm, D_MODEL), lambda i, n: (i, 0)),
            pl.BlockSpec((1, D_MODEL), lambda i, n: (0, 0)),
            pl.BlockSpec((D_MODEL, tn), lambda i, n: (0, n)),
            pl.BlockSpec((D_MODEL, LANES), lambda i, n: (0, 0)),
            pl.BlockSpec((D_MODEL, LANES), lambda i, n: (0, 0)),
        ],
        out_specs=[
            pl.BlockSpec((tm, tn), lambda i, n: (i, n)),
            pl.BlockSpec((tm, LANES), lambda i, n: (i, 0)),
        ],
        out_shape=[jax.ShapeDtypeStruct((t, cols), F32), jax.ShapeDtypeStruct((t, LANES), F32)],
        scratch_shapes=[pltpu.VMEM((tm, D_MODEL), BF16)],
        compiler_params=_params(("parallel", "arbitrary")),
        name="inproj",
    )(x, g, w_main, w_dt_hi, w_dt_lo)


_HIST = SUBLANES - (CONV_W - 1)


def _ssd_kernel(*refs, q_real, q_pad, nc, has_init, n_prev):
    prev_refs = ()
    if has_init:
        (xbc_ref, z_ref, dt_ref, h0_ref, cbuf_ref, cw_ref, cb_ref, dtb_ref, alog_ref, dsk_ref,
         ng_ref, e_ref) = refs[:12]
        prev_refs = refs[12:12 + n_prev]
        y_ref, hout_ref, ext_scr, h_scr, dt_scr, z_scr = refs[12 + n_prev:]
    else:
        (xbc_ref, x_ref, dt_ref, cw_ref, cb_ref, dtb_ref, alog_ref, dsk_ref,
         ng_ref, e_ref, gm_ref, wz_ref, y_ref, hout_ref, ext_scr, h_scr, dt_scr) = refs
    c = pl.program_id(1)

    @pl.when(c == 0)
    def _():
        ext_scr[...] = jnp.zeros_like(ext_scr)
        dt_scr[...] = jnp.zeros_like(dt_scr)
        if has_init:
            z_scr[...] = jnp.zeros_like(z_scr)
        if has_init:
            h_scr[...] = h0_ref[0, 0]
            ext_scr[_HIST:SUBLANES, :] = cbuf_ref[0, 0]
        else:
            h_scr[...] = jnp.zeros_like(h_scr)

    ext_scr[SUBLANES:SUBLANES + q_real, :] = xbc_ref[0]
    ext = ext_scr[...]
    acc = cb_ref[...] + cw_ref[CONV_W - 1:CONV_W, :] * ext[SUBLANES:SUBLANES + q_pad, :]
    for w in range(CONV_W - 1):
        acc = acc + cw_ref[w:w + 1, :] * pltpu.roll(ext, CONV_W - 1 - w, axis=0)[SUBLANES:SUBLANES + q_pad, :]
    if nc > 1:
        ext_scr[_HIST:SUBLANES, :] = ext_scr[SUBLANES + q_real - (CONV_W - 1):SUBLANES + q_real, :]
    u = _silu(acc)
    xs = u[:, :D_INNER]
    b_bf = u[:, D_INNER:D_INNER + SSD_GROUPS * D_STATE].astype(BF16)
    c_bf = u[:, D_INNER + SSD_GROUPS * D_STATE:].astype(BF16)

    dt_scr[0:q_real, :] = dt_ref[0]
    if has_init:
        z_scr[0:q_real, :] = z_ref[0]
    else:
        h_bf = _rms(x_ref[0], gm_ref[...]).astype(BF16)
    rows =lax.broadcasted_iota(jnp.int32, (q_pad, LANES), 0)
    dtr = dt_scr[...] + dtb_ref[...]
    softplus = jnp.maximum(dtr, 0.0) + jnp.log1p(jnp.exp(-jnp.abs(dtr)))
    dt = jnp.where(rows < q_real, softplus, 0.0)
    da = dt * (-jnp.exp(alog_ref[...]))

    ri = lax.broadcasted_iota(jnp.int32, (q_pad, q_pad), 0)
    ci = lax.broadcasted_iota(jnp.int32, (q_pad, q_pad), 1)
    causal = ri >= ci
    a_cum = _sel_left(causal.astype(BF16), da)
    a_last = a_cum[q_pad - 1:q_pad, :]
    if q_pad < LANES:
        a_sq = jnp.concatenate([a_cum, jnp.zeros((LANES - q_pad, LANES), F32)], axis=0)
    else:
        a_sq = a_cum
    a_cum_t = a_sq.T

    e01 = e_ref[...]
    dt_x = _sel_right(dt, e01, pieces=2)
    eac_x = _sel_right(jnp.exp(a_cum), e01, pieces=2)
    dte_x = _sel_right(jnp.exp(a_last - a_cum), e01, pieces=2)
    xw = xs * dt_x
    xw_bf = xw.astype(BF16)
    xwd_bf = (xw * dte_x).astype(BF16)

    dec_rows = jnp.broadcast_to(jnp.exp(a_cum_t[:, q_pad - 1:q_pad]), (LANES, LANES))

    lane = lax.broadcasted_iota(jnp.int32, (q_pad, LANES), 1)
    for g in range(SSD_GROUPS):
        cg = c_bf[:, g * D_STATE:(g + 1) * D_STATE]
        bg = b_bf[:, g * D_STATE:(g + 1) * D_STATE]
        cb = lax.dot_general(cg, bg, _NT, preferred_element_type=F32)
        hg = h_scr[g * GROUP_W:(g + 1) * GROUP_W, :]
        y_off = lax.dot_general(cg, hg.astype(BF16), _NT, preferred_element_type=F32)
        parts = []
        for k in range(4):
            slab = g * 4 + k
            xwp = xw_bf[:, slab * LANES:(slab + 1) * LANES]
            pair = []
            for h in (2 * slab, 2 * slab + 1):
                seg = a_cum[:, h:h + 1] - a_cum_t[h:h + 1, :q_pad]
                decay_in = jnp.exp(jnp.where(causal, seg, -jnp.inf))
                pair.append(jnp.dot((cb * decay_in).astype(BF16), xwp, preferred_element_type=F32))
            parts.append(jnp.where(lane < SSD_HEADDIM, pair[0], pair[1]))
        y_diag = jnp.concatenate(parts, axis=1)
        s_new = lax.dot_general(xwd_bf[:, g * GROUP_W:(g + 1) * GROUP_W], bg, _TN,
                                preferred_element_type=F32)
        for hh in range(GROUP_W // SSD_HEADDIM):
            head = g * (GROUP_W // SSD_HEADDIM) + hh
            r0, r1 = hh * SSD_HEADDIM, (hh + 1) * SSD_HEADDIM
            h_scr[g * GROUP_W + r0:g * GROUP_W + r1, :] = (hg[r0:r1, :] * dec_rows[head:head + 1, :]
                                                          + s_new[r0:r1, :])

        gw = slice(g * GROUP_W, (g + 1) * GROUP_W)
        y = y_diag + y_off * eac_x[:, gw] + dsk_ref[:, gw] * xs[:, gw]
        if has_init:
            zg = z_scr[:, gw]
        else:
            zg = jnp.dot(h_bf, wz_ref[:, gw], preferred_element_type=F32)
        ug = y * _silu(zg)
        out = ug * lax.rsqrt(jnp.mean(ug * ug, axis=-1, keepdims=True) + EPS) * ng_ref[:, gw]
        y_ref[0, :, gw] = out[0:q_real, :]

    @pl.when(c == nc - 1)
    def _():
        for k in range(n_prev):
            hout_ref[k, 0] = prev_refs[k][0]
        hout_ref[n_prev, 0] = h_scr[...]


def _ssd(p3, dt3, consts, x3=None, gate_w=None, state=None, collect=()):
    b, L, _ = p3.shape
    q_real = min(SSD_CHUNK, L)
    q_pad = -(-q_real // SUBLANES) * SUBLANES
    nc = L // q_real
    has_init = state is not None
    assert has_init or q_real == q_pad
    cw, cbias, dtb, alog, dsk, ng, e01 = consts

    def const(shape):
        return pl.BlockSpec(shape, lambda i, c: (0,) * len(shape))

    xbc_spec = pl.BlockSpec((1, q_real, CONV_DIM), lambda i, c: (i, c, COL_XBC // CONV_DIM))
    dt_spec = pl.BlockSpec((1, q_real, LANES), lambda i, c: (i, c, 0))
    shared = [const((CONV_W, CONV_DIM)), const((1, CONV_DIM)), const((1, LANES)), const((1, LANES)),
              const((1, D_INNER)), const((1, D_INNER)), const((LANES, D_INNER))]
    scratch = [pltpu.VMEM((SUBLANES + q_pad, CONV_DIM), F32), pltpu.VMEM((D_INNER, D_STATE), F32),
               pltpu.VMEM((q_pad, LANES), F32)]
    if has_init:
        h0, cbuf, layer = state
        in_specs = [xbc_spec, pl.BlockSpec((1, q_real, D_INNER), lambda i, c: (i, c, COL_Z_S // D_INNER)), dt_spec,
                    pl.BlockSpec((1, 1, D_INNER, D_STATE), lambda i, c: (layer, i, 0, 0)),
                    pl.BlockSpec((1, 1, CONV_W - 1, CONV_DIM), lambda i, c: (layer, i, 0, 0))] + shared
        in_specs += [pl.BlockSpec((1, D_INNER, D_STATE), lambda i, c: (i, 0, 0)) for _ in collect]
        args = [p3, p3, dt3, h0, cbuf, cw, cbias, dtb, alog, dsk, ng, e01, *collect]
        scratch.append(pltpu.VMEM((q_pad, D_INNER), F32))
    else:
        assert not collect
        gm, wz = gate_w
        in_specs = ([xbc_spec, pl.BlockSpec((1, q_real, D_MODEL), lambda i, c: (i, c, 0)), dt_spec] + shared
                    + [const((1, D_MODEL)), const((D_MODEL, D_INNER))])
        args = [p3, x3, dt3, cw, cbias, dtb, alog, dsk, ng, e01, gm, wz]
    n_out = len(collect) + 1
    return pl.pallas_call(
        functools.partial(_ssd_kernel, q_real=q_real, q_pad=q_pad, nc=nc, has_init=has_init,
                          n_prev=len(collect)),
        grid=(b, nc),
        in_specs=in_specs,
        out_specs=[
            pl.BlockSpec((1, q_real, D_INNER), lambda i, c: (i, c, 0)),
            pl.BlockSpec((n_out, 1, D_INNER, D_STATE), lambda i, c: (0, i, 0, 0)),
        ],
        out_shape=[jax.ShapeDtypeStruct((b, L, D_INNER), F32),
                   jax.ShapeDtypeStruct((n_out, b, D_INNER, D_STATE), F32)],
        scratch_shapes=scratch,
        compiler_params=_params(("parallel", "arbitrary")),
        name="ssd_s" if has_init else "ssd_p",
    )(*args)


MOBA_P_HEADS = 8
MOBA_P_VMEM = 58 * 1024 * 1024
GATE_ROWS = 2 * SUBLANES


def _moba_p_kernel(slopes_ref, q_ref, k_ref, v_ref, o_ref, kmh_scr, kml_scr, k_scr, vt_scr, bias_scr,
                   sel_scr, qs_scr, acc_scr, m_scr, l_scr, *, nb):
    hg = pl.program_id(1)
    j = pl.program_id(2)
    blk = MOBA_BLOCK
    G = MOBA_P_HEADS
    hd = ATT_HEAD_DIM

    @pl.when(j == 0)
    def _():
        krow = lax.broadcasted_iota(jnp.int32, (blk, blk), 0).astype(F32)
        for g in range(G):
            means = []
            for n in range(nb):
                kn = k_ref[0, n * blk:(n + 1) * blk, g * hd:(g + 1) * hd]
                means.append(jnp.sum(kn, axis=0, keepdims=True) * (1.0 / blk))
                k_scr[n, g] = kn.astype(BF16)
                vt_scr[n, g] = v_ref[0, n * blk:(n + 1) * blk, g * hd:(g + 1) * hd].T.astype(BF16)
            km = jnp.concatenate(means + [jnp.zeros((GATE_ROWS - nb, hd), F32)], axis=0)
            km_hi = km.astype(BF16)
            kmh_scr[g] = km_hi
            kml_scr[g] = (km - km_hi.astype(F32)).astype(BF16)
            bias_scr[g] = slopes_ref[hg * G + g] * krow

    rowid = lax.broadcasted_iota(jnp.int32, (SUBLANES, blk), 0)
    krow_i = lax.broadcasted_iota(jnp.int32, (blk, blk), 0)
    qcol_i = lax.broadcasted_iota(jnp.int32, (blk, blk), 1)
    causal = krow_i <= qcol_i
    scale = hd ** -0.5
    c_own = (j * blk).astype(F32)

    qs_all, gates, s_own = [], [], []
    for g in range(G):
        q = q_ref[0, :, g * hd:(g + 1) * hd]
        q_hi = q.astype(BF16)
        q_lo = (q - q_hi.astype(F32)).astype(BF16)
        qs = (q * scale).astype(BF16)
        qs_all.append(qs)
        gates.append((lax.dot_general(kmh_scr[g], q_hi, _NT, preferred_element_type=F32)
                      + lax.dot_general(kmh_scr[g], q_lo, _NT, preferred_element_type=F32)
                      + lax.dot_general(kml_scr[g], q_hi, _NT, preferred_element_type=F32))[0:SUBLANES, :])
        s_own.append(lax.dot_general(k_scr[j, g], qs, _NT, preferred_element_type=F32))
    sels, m0s, l0s, p_own = [], [], [], []
    for g in range(G):
        gate = jnp.where(rowid < j, gates[g], -jnp.inf)
        cnt = jnp.zeros((SUBLANES, blk), F32)
        for m in range(nb):
            gm = gate[m:m + 1, :]
            beats = (gm > gate) | ((gm == gate) & (m < rowid))
            cnt = cnt + jnp.where(beats, 1.0, 0.0)
        sels.append(jnp.where((cnt < MOBA_TOPK) & (rowid < j), 1.0, 0.0))
        s = jnp.where(causal, s_own[g] + bias_scr[g], -jnp.inf)
        smax = jnp.max(s, axis=0, keepdims=True)
        p = jnp.exp(s - smax)
        m0s.append(smax + slopes_ref[hg * G + g] * jnp.full((1, blk), c_own, F32))
        l0s.append(jnp.sum(p, axis=0, keepdims=True))
        p_own.append(p.astype(BF16))
    for g in range(G):
        acc_scr[g] = jnp.dot(vt_scr[j, g], p_own[g], preferred_element_type=F32)
        sel_scr[g], qs_scr[g], m_scr[g], l_scr[g] = sels[g], qs_all[g], m0s[g], l0s[g]

    def body(n, carry):
        c_n = jnp.full((1, blk), (n * blk).astype(F32), F32)
        scores = [lax.dot_general(k_scr[n, g], qs_scr[g], _NT, preferred_element_type=F32) for g in range(G)]
        stats, probs = [], []
        for g in range(G):
            c_blk = slopes_ref[hg * G + g] * c_n
            s = scores[g] + bias_scr[g]
            m_i = m_scr[g]
            m_use = jnp.maximum(m_i, jnp.max(s, axis=0, keepdims=True) + c_blk)
            p = jnp.exp(s - (m_use - c_blk))
            psum = jnp.sum(p, axis=0, keepdims=True)
            chosen = sel_scr[g, pl.ds(n, 1), :] > 0.0
            alpha = jnp.where(chosen, jnp.exp(m_i - m_use), 1.0)
            keep = jnp.where(chosen, 1.0, 0.0)
            stats.append((jnp.where(chosen, m_use, m_i), alpha * l_scr[g] + keep * psum, alpha, keep))
            probs.append(p.astype(BF16))
        for g in range(G):
            pv = jnp.dot(vt_scr[n, g], probs[g], preferred_element_type=F32)
            m_new, l_new, alpha, keep = stats[g]
            acc_scr[g] = alpha * acc_scr[g] + keep * pv
            m_scr[g], l_scr[g] = m_new, l_new
        return carry

    lax.fori_loop(0, j, body, 0)
    for g in range(G):
        o_ref[0, :, g * hd:(g + 1) * hd] = (acc_scr[g] / l_scr[g]).T


def _moba_p(p3, slopes):
    b, L, _ = p3.shape
    nb = L // MOBA_BLOCK
    G = MOBA_P_HEADS
    gw = G * ATT_HEAD_DIM
    assert nb <= SUBLANES and ATT_HEADS % G == 0 and COL_Q % gw == 0 and COL_K % gw == 0 and COL_V % gw == 0
    cq, ck, cv = COL_Q // gw, COL_K // gw, COL_V // gw
    return pl.pallas_call(
        functools.partial(_moba_p_kernel, nb=nb),
        grid=(b, ATT_HEADS // G, nb),
        in_specs=[
            pl.BlockSpec(memory_space=pltpu.SMEM),
            pl.BlockSpec((1, MOBA_BLOCK, gw), lambda i, h, j: (i, j, cq + h)),
            pl.BlockSpec((1, L, gw), lambda i, h, j: (i, 0, ck + h)),
            pl.BlockSpec((1, L, gw), lambda i, h, j: (i, 0, cv + h)),
        ],
        out_specs=pl.BlockSpec((1, MOBA_BLOCK, gw), lambda i, h, j: (i, j, h)),
        out_shape=jax.ShapeDtypeStruct((b, L, ATT_WIDTH), F32),
        scratch_shapes=[
            pltpu.VMEM((G, GATE_ROWS, ATT_HEAD_DIM), BF16),
            pltpu.VMEM((G, GATE_ROWS, ATT_HEAD_DIM), BF16),
            pltpu.VMEM((nb, G, MOBA_BLOCK, ATT_HEAD_DIM), BF16),
            pltpu.VMEM((nb, G, ATT_HEAD_DIM, MOBA_BLOCK), BF16),
            pltpu.VMEM((G, MOBA_BLOCK, MOBA_BLOCK), F32),
            pltpu.VMEM((G, SUBLANES, MOBA_BLOCK), F32),
            pltpu.VMEM((G, MOBA_BLOCK, ATT_HEAD_DIM), BF16),
            pltpu.VMEM((G, ATT_HEAD_DIM, MOBA_BLOCK), F32),
            pltpu.VMEM((G, 1, MOBA_BLOCK), F32),
            pltpu.VMEM((G, 1, MOBA_BLOCK), F32),
        ],
        compiler_params=_params(("parallel", "parallel", "arbitrary"), MOBA_P_VMEM),
        name="moba_p",
    )(slopes, p3, p3, p3)


MOBA_S_BLOCKS = 4


def _moba_s_kernel(pt_ref, q_ref, kn_ref, vn_ref, *rest, nb, lq, past_len):
    npg = 2 * MOBA_S_BLOCKS
    k_refs, v_refs = rest[:npg], rest[npg:2 * npg]
    o_ref, acc_scr, m_scr, l_scr, g_scr = rest[2 * npg:]
    step = pl.program_id(1)
    rows = lq * ATT_HEADS
    keys = PAGE_SIZE * ATT_HEADS
    rid = lax.broadcasted_iota(jnp.int32, (rows, 1), 0)
    head = rid % ATT_HEADS
    t_pos = (past_len + rid // ATT_HEADS).astype(F32)
    slope = jnp.exp2(-(head + 1).astype(F32) * (8.0 / ATT_HEADS))
    scale = ATT_HEAD_DIM ** -0.5

    q = q_ref[0]
    q_bf = (q * scale).astype(BF16)
    col = lax.broadcasted_iota(jnp.int32, (rows, keys), 1)
    in_page = jnp.where((col % ATT_HEADS) == head, slope * ((col // ATT_HEADS).astype(F32) - t_pos), -jnp.inf)

    raw, gates = [], []
    for i in range(MOBA_S_BLOCKS):
        k0 = k_refs[2 * i][0, 0]
        k1 = k_refs[2 * i + 1][0, 0]
        kmean = (jnp.sum(k0, axis=0) + jnp.sum(k1, axis=0)) * (1.0 / MOBA_BLOCK)
        gates.append(jnp.sum(q * jnp.concatenate([kmean] * lq, axis=0), axis=-1, keepdims=True))
        raw.append([lax.dot_general(q_bf, kh.reshape(keys, ATT_HEAD_DIM).astype(BF16), _NT,
                                    preferred_element_type=F32) for kh in (k0, k1)])
    stats, probs = [], []
    for i in range(MOBA_S_BLOCKS):
        n = step * MOBA_S_BLOCKS + i
        s = [raw[i][h] + in_page + slope * (n * MOBA_BLOCK + h * PAGE_SIZE).astype(F32) for h in range(2)]
        m_n = jnp.maximum(jnp.max(s[0], axis=-1, keepdims=True), jnp.max(s[1], axis=-1, keepdims=True))
        p = [jnp.exp(s[h] - m_n) for h in range(2)]
        l_n = jnp.sum(p[0], axis=-1, keepdims=True) + jnp.sum(p[1], axis=-1, keepdims=True)
        stats.append((m_n, l_n))
        probs.append([ph.astype(BF16) for ph in p])
    for i in range(MOBA_S_BLOCKS):
        n = step * MOBA_S_BLOCKS + i
        acc_scr[n] = sum(jnp.dot(probs[i][h], v_refs[2 * i + h][0, 0].reshape(keys, ATT_HEAD_DIM).astype(BF16),
                                 preferred_element_type=F32) for h in range(2))
        m_scr[n] = jnp.broadcast_to(stats[i][0], (rows, LANES))
        l_scr[n] = jnp.broadcast_to(stats[i][1], (rows, LANES))
        g_scr[n] = jnp.broadcast_to(gates[i], (rows, LANES))

    @pl.when(step == nb // MOBA_S_BLOCKS - 1)
    def _():
        qidx = rid // ATT_HEADS
        s_own = []
        for kk in range(lq):
            kn = jnp.concatenate([kn_ref[0, kk * ATT_HEADS:(kk + 1) * ATT_HEADS, :]] * lq, axis=0)
            s = jnp.sum(q * kn, axis=-1, keepdims=True) * scale
            s = s - slope * (t_pos - float(past_len + kk))
            s_own.append(jnp.where(kk <= qidx, s, -jnp.inf))
        gates = [g_scr[i][:, 0:1] for i in range(nb)]
        ms = [m_scr[i][:, 0:1] for i in range(nb)]
        sel = []
        for i in range(nb):
            cnt = jnp.zeros((rows, 1), F32)
            for m in range(nb):
                beats = (gates[m] > gates[i]) | ((gates[m] == gates[i]) & (m < i))
                cnt = cnt + jnp.where(beats, 1.0, 0.0)
            sel.append(cnt < MOBA_TOPK)
        m_tot = s_own[0]
        for kk in range(1, lq):
            m_tot = jnp.maximum(m_tot, s_own[kk])
        for i in range(nb):
            m_tot = jnp.maximum(m_tot, jnp.where(sel[i], ms[i], -jnp.inf))
        l_tot = jnp.zeros((rows, 1), F32)
        acc = jnp.zeros((rows, ATT_HEAD_DIM), F32)
        for kk in range(lq):
            w = jnp.exp(s_own[kk] - m_tot)
            l_tot = l_tot + w
            acc = acc + w * jnp.concatenate([vn_ref[0, kk * ATT_HEADS:(kk + 1) * ATT_HEADS, :]] * lq, axis=0)
        for i in range(nb):
            w = jnp.where(sel[i], jnp.exp(ms[i] - m_tot), 0.0)
            l_tot = l_tot + w * l_scr[i][:, 0:1]
            acc = acc + w * acc_scr[i]
        o_ref[0] = acc / l_tot


def _moba_s(q2, kn2, vn2, cache_k, cache_v, page_table, layer):
    b, rows, _ = q2.shape
    lq = rows // ATT_HEADS
    n_pages = page_table.shape[1]
    past_len = n_pages * PAGE_SIZE
    nb = past_len // MOBA_BLOCK
    ppb = MOBA_BLOCK // PAGE_SIZE
    assert ppb == 2 and nb >= MOBA_TOPK and lq <= MOBA_BLOCK and nb % MOBA_S_BLOCKS == 0
    npg = ppb * MOBA_S_BLOCKS

    def page(off):
        return pl.BlockSpec((1, 1, PAGE_SIZE, ATT_HEADS, ATT_HEAD_DIM),
                            lambda i, n, pt: (layer, pt[i, npg * n + off], 0, 0, 0))

    def new(_):
        return pl.BlockSpec((1, rows, ATT_HEAD_DIM), lambda i, n, pt: (i, 0, 0))

    pages = [page(off) for off in range(npg)]
    grid_spec = pltpu.PrefetchScalarGridSpec(
        num_scalar_prefetch=1,
        grid=(b, nb // MOBA_S_BLOCKS),
        in_specs=[new(0), new(1), new(2)] + pages + pages,
        out_specs=pl.BlockSpec((1, rows, ATT_HEAD_DIM), lambda i, n, pt: (i, 0, 0)),
        scratch_shapes=[
            pltpu.VMEM((nb, rows, ATT_HEAD_DIM), F32),
            pltpu.VMEM((nb, rows, LANES), F32),
            pltpu.VMEM((nb, rows, LANES), F32),
            pltpu.VMEM((nb, rows, LANES), F32),
        ],
    )
    return pl.pallas_call(
        functools.partial(_moba_s_kernel, nb=nb, lq=lq, past_len=past_len),
        grid_spec=grid_spec,
        out_shape=jax.ShapeDtypeStruct((b, rows, ATT_HEAD_DIM), F32),
        compiler_params=_params(("parallel", "arbitrary")),
        name="moba_s",
    )(page_table, q2, kn2, vn2, *([cache_k] * npg), *([cache_v] * npg))


def _merge_kernel(x_ref, ya_ref, yb_ref, ga_ref, gb_ref, wa_ref, wb_ref, wo_ref, o_ref):
    a = jnp.dot(ya_ref[...].astype(BF16), wa_ref[...], preferred_element_type=F32)
    b = jnp.dot(yb_ref[...].astype(BF16), wb_ref[...], preferred_element_type=F32)
    m = jax.nn.sigmoid(ga_ref[...]) * a + jax.nn.sigmoid(gb_ref[...]) * b
    o_ref[...] = x_ref[...] + jnp.dot(m.astype(BF16), wo_ref[...], preferred_element_type=F32)


def _merge(x, ya, yb, p, wpa, wpb, wo):
    t = x.shape[0]
    tm = min(512, t)
    return pl.pallas_call(
        _merge_kernel,
        grid=(t // tm,),
        in_specs=[
            pl.BlockSpec((tm, D_MODEL), lambda i: (i, 0)),
            pl.BlockSpec((tm, D_INNER), lambda i: (i, 0)),
            pl.BlockSpec((tm, ATT_WIDTH), lambda i: (i, 0)),
            pl.BlockSpec((tm, D_MODEL), lambda i: (i, COL_GA // D_MODEL)),
            pl.BlockSpec((tm, D_MODEL), lambda i: (i, COL_GB // D_MODEL)),
            pl.BlockSpec((D_INNER, D_MODEL), lambda i: (0, 0)),
            pl.BlockSpec((ATT_WIDTH, D_MODEL), lambda i: (0, 0)),
            pl.BlockSpec((D_MODEL, D_MODEL), lambda i: (0, 0)),
        ],
        out_specs=pl.BlockSpec((tm, D_MODEL), lambda i: (i, 0)),
        out_shape=jax.ShapeDtypeStruct((t, D_MODEL), F32),
        compiler_params=_params(("parallel",)),
        name="merge",
    )(x, ya, yb, p, p, wpa, wpb, wo)


def _kvout_kernel(*refs, depth):
    ok_ref, ov_ref = refs[2 * depth:]
    for l in range(depth):
        ok_ref[l] = refs[2 * l][...].reshape(ok_ref.shape[1:])
        ov_ref[l] = refs[2 * l + 1][...].reshape(ov_ref.shape[1:])


def _kvout(ps):
    depth = len(ps)
    t = ps[0].shape[0]
    tm = min(512, t)
    in_specs, args = [], []
    for p in ps:
        in_specs += [pl.BlockSpec((tm, ATT_WIDTH), lambda i: (i, COL_K // ATT_WIDTH)),
                     pl.BlockSpec((tm, ATT_WIDTH), lambda i: (i, COL_V // ATT_WIDTH))]
        args += [p, p]
    out_spec = pl.BlockSpec((depth, tm, ATT_HEADS, ATT_HEAD_DIM), lambda i: (0, i, 0, 0))
    out_shape = jax.ShapeDtypeStruct((depth, t, ATT_HEADS, ATT_HEAD_DIM), F32)
    return pl.pallas_call(
        functools.partial(_kvout_kernel, depth=depth),
        grid=(t // tm,),
        in_specs=in_specs,
        out_specs=[out_spec, out_spec],
        out_shape=[out_shape, out_shape],
        compiler_params=_params(("parallel",)),
        name="kvout",
    )(*args)


def _pad_lanes(v):
    return jnp.pad(v.astype(F32), (0, LANES - v.shape[0])).reshape(1, LANES)


def kernel(x_prompt, x_sample, cache_k, cache_v, state_ssm, state_conv, page_table, ln_ffn1, w_ffn1_in,
           w_ffn1_out, ln_mix, w_in, conv_w, conv_b, dt_bias, a_log, d_skip, ssd_norm, w_proj_a, w_proj_b,
           w_out, ln_ffn2, w_ffn2_in, w_ffn2_out, ln_final):
    depth = w_in.shape[0]
    bp, seq, _ = x_prompt.shape
    bs, lq, _ = x_sample.shape
    slopes = jnp.exp2(-8.0 * jnp.arange(1, ATT_HEADS + 1, dtype=F32) / ATT_HEADS)
    head_of = jnp.arange(D_INNER, dtype=jnp.int32) // SSD_HEADDIM
    e01 = (jnp.arange(LANES, dtype=jnp.int32)[:, None] == head_of[None, :]).astype(BF16)
    h0_all = state_ssm.reshape(depth, bs, D_INNER, D_STATE)

    o_z, o_xbc, o_dt = 0, D_INNER, D_INNER + CONV_DIM
    o_q = o_dt + SSD_HEADS

    xp = x_prompt.reshape(bp * seq, D_MODEL)
    xs = x_sample.reshape(bs * lq, D_MODEL)
    outs = {k: [] for k in ("hp", "cp", "ks", "vs", "hs", "cs")}
    pps = []
    row = lambda v: v.astype(F32).reshape(1, -1)
    heads_rows = lambda a: a.reshape(bs, lq * ATT_HEADS, ATT_HEAD_DIM)
    for l in range(depth):
        wl = w_in[l]
        w_main = jnp.concatenate([wl[:, o_xbc:o_dt], wl[:, o_q:]], axis=1).astype(BF16)
        w_z = wl[:, o_z:o_xbc].astype(BF16)
        w_dt = jnp.pad(wl[:, o_dt:o_q], ((0, 0), (0, LANES - SSD_HEADS)))
        w_dt_hi = w_dt.astype(BF16)
        w_dt_lo = (w_dt - w_dt_hi.astype(F32)).astype(BF16)
        w1i, w1o = w_ffn1_in[l].astype(BF16), w_ffn1_out[l].astype(BF16)
        w2i, w2o = w_ffn2_in[l].astype(BF16), w_ffn2_out[l].astype(BF16)
        wpa, wpb, wo = w_proj_a[l].astype(BF16), w_proj_b[l].astype(BF16), w_out[l].astype(BF16)
        consts = (conv_w[l], row(conv_b[l]), _pad_lanes(dt_bias[l]), _pad_lanes(a_log[l]),
                  row(jnp.repeat(d_skip[l], SSD_HEADDIM)), row(ssd_norm[l]), e01)
        g1, gm, g2 = row(ln_ffn1[l]), row(ln_mix[l]), row(ln_ffn2[l])

        xp = _ffn(xp, g1, w1i, w1o)
        xs = _ffn(xs, g1, w1i, w1o)
        pp, dtp = _inproj(xp, gm, w_main, w_dt_hi, w_dt_lo)
        ps, dts = _inproj(xs, gm, jnp.concatenate([w_main, w_z], axis=1), w_dt_hi, w_dt_lo)
        pp3 = pp.reshape(bp, seq, P_COLS)
        ps3 = ps.reshape(bs, lq, COL_Z_S + D_INNER)
        pps.append(pp)
        ks_new = ps3[:, :, COL_K:COL_K + ATT_WIDTH].reshape(bs, lq, ATT_HEADS, ATT_HEAD_DIM)
        vs_new = ps3[:, :, COL_V:COL_V + ATT_WIDTH].reshape(bs, lq, ATT_HEADS, ATT_HEAD_DIM)

        ya_p, h_p = _ssd(pp3, dtp.reshape(bp, seq, LANES), consts, x3=xp.reshape(bp, seq, D_MODEL),
                         gate_w=(gm, w_z))
        ya_s, h_s = _ssd(ps3, dts.reshape(bs, lq, LANES), consts, state=(h0_all, state_conv, l),
                         collect=tuple(outs["hs"]) if l == depth - 1 else ())
        yb_p = _moba_p(pp3, slopes)
        yb_s = _moba_s(heads_rows(ps3[:, :, COL_Q:COL_Q + ATT_WIDTH]), heads_rows(ks_new), heads_rows(vs_new),
                       cache_k, cache_v, page_table, l)

        xp = _merge(xp, ya_p.reshape(bp * seq, D_INNER), yb_p.reshape(bp * seq, ATT_WIDTH), pp, wpa, wpb, wo)
        xs = _merge(xs, ya_s.reshape(bs * lq, D_INNER), yb_s.reshape(bs * lq, ATT_WIDTH), ps, wpa, wpb, wo)
        g_last = row(ln_final) if l == depth - 1 else None
        xp = _ffn(xp, g2, w2i, w2o, g_last)
        xs = _ffn(xs, g2, w2i, w2o, g_last)

        outs["hp"].append(h_p.reshape(bp, SSD_HEADS, SSD_HEADDIM, D_STATE))
        outs["cp"].append(pp3[:, seq - (CONV_W - 1):, COL_XBC:COL_XBC + CONV_DIM])
        outs["ks"].append(ks_new)
        outs["vs"].append(vs_new)
        if l == depth - 1:
            ssm_sample = h_s.reshape(depth, bs, SSD_HEADS, SSD_HEADDIM, D_STATE)
        else:
            outs["hs"].append(h_s[0])
        if lq >= CONV_W - 1:
            cs = ps3[:, lq - (CONV_W - 1):, COL_XBC:COL_XBC + CONV_DIM]
        else:
            cs = jnp.concatenate([state_conv[l], ps3[:, :, COL_XBC:COL_XBC + CONV_DIM]], axis=1)[:, -(CONV_W - 1):]
        outs["cs"].append(cs)

    y_prompt = xp.reshape(bp, seq, D_MODEL)
    y_sample = xs.reshape(bs, lq, D_MODEL)
    k_prompt, v_prompt = _kvout(pps)
    kv5 = lambda a: a.reshape(depth, bp, seq, ATT_HEADS, ATT_HEAD_DIM)
    st = lambda k: jnp.stack(outs[k])
    return (y_prompt, y_sample, kv5(k_prompt), kv5(v_prompt), st("hp"), st("cp"), st("ks"), st("vs"), ssm_sample,
            st("cs"))
```

```python
import functools

import jax
import jax.numpy as jnp
from jax import lax
from jax.experimental import pallas as pl
from jax.experimental.pallas import tpu as pltpu

F32 = jnp.float32
BF16 = jnp.bfloat16

D_MODEL = 1024
D_INNER = 2048
SSD_HEADDIM = 64
SSD_HEADS = 32
SSD_GROUPS = 4
GROUP_W = D_INNER // SSD_GROUPS
D_STATE = 128
CONV_W = 4
CONV_DIM = 3072
SSD_CHUNK = 128
ATT_HEADS = 8
ATT_HEAD_DIM = 128
ATT_WIDTH = 1024
MOBA_BLOCK = 256
MOBA_TOPK = 3
PAGE_SIZE = 128
D_FF = 2816
FFN_SCALE = 0.5
EPS = 1e-6

LANES = 128
SUBLANES = 8
VMEM_LIMIT = 48 * 1024 * 1024

P_COLS = 8192
COL_Z_S = 8192
COL_XBC = 0
COL_Q = 3072
COL_K = 4096
COL_V = 5120
COL_GA = 6144
COL_GB = 7168

_NT = (((1,), (1,)), ((), ()))
_TN = (((0,), (0,)), ((), ()))


def _params(sem, vmem_limit=VMEM_LIMIT):
    return pltpu.CompilerParams(dimension_semantics=sem, vmem_limit_bytes=vmem_limit)


def _rms(x, g):
    return x * lax.rsqrt(jnp.mean(x * x, axis=-1, keepdims=True) + EPS) * g


def _silu(x):
    h = 0.5 * x
    return h * jnp.tanh(h) + h


def _split3(x):
    x1 = x.astype(BF16)
    r = x - x1.astype(F32)
    x2 = r.astype(BF16)
    r = r - x2.astype(F32)
    return x1, x2, r.astype(BF16)


def _sel_left(a01, x, pieces=3):
    return sum(jnp.dot(a01, p, preferred_element_type=F32) for p in _split3(x)[:pieces])


def _sel_right(x, e01, pieces=3):
    return sum(jnp.dot(p, e01, preferred_element_type=F32) for p in _split3(x)[:pieces])


def _ffn_kernel(x_ref, g_ref, wa_ref, wb_ref, wo_ref, gout_ref, o_ref, h_scr, acc_scr, *, nf, norm_out):
    f = pl.program_id(1)

    @pl.when(f == 0)
    def _():
        h_scr[...] = _rms(x_ref[...], g_ref[...]).astype(BF16)
        acc_scr[...] = jnp.zeros_like(acc_scr)

    h = h_scr[...]
    a = jnp.dot(h, wa_ref[...], preferred_element_type=F32)
    b = jnp.dot(h, wb_ref[...], preferred_element_type=F32)
    acc_scr[...] += jnp.dot((_silu(a) * b).astype(BF16), wo_ref[...], preferred_element_type=F32)

    @pl.when(f == nf - 1)
    def _():
        y = x_ref[...] + FFN_SCALE * acc_scr[...]
        o_ref[...] = _rms(y, gout_ref[...]) if norm_out else y


def _ffn(x, g, w_in, w_out, g_out=None):
    t = x.shape[0]
    tm = min(512, t)
    nf = 2
    tf = D_FF // nf
    norm_out = g_out is not None
    return pl.pallas_call(
        functools.partial(_ffn_kernel, nf=nf, norm_out=norm_out),
        grid=(t // tm, nf),
        in_specs=[
            pl.BlockSpec((tm, D_MODEL), lambda i, f: (i, 0)),
            pl.BlockSpec((1, D_MODEL), lambda i, f: (0, 0)),
            pl.BlockSpec((D_MODEL, tf), lambda i, f: (0, f)),
            pl.BlockSpec((D_MODEL, tf), lambda i, f: (0, f + nf)),
            pl.BlockSpec((tf, D_MODEL), lambda i, f: (f, 0)),
            pl.BlockSpec((1, D_MODEL), lambda i, f: (0, 0)),
        ],
        out_specs=pl.BlockSpec((tm, D_MODEL), lambda i, f: (i, 0)),
        out_shape=jax.ShapeDtypeStruct((t, D_MODEL), F32),
        scratch_shapes=[pltpu.VMEM((tm, D_MODEL), BF16), pltpu.VMEM((tm, D_MODEL), F32)],
        compiler_params=_params(("parallel", "arbitrary")),
        name="ffn",
    )(x, g, w_in, w_in, w_out, g_out if norm_out else g)


def _inproj_kernel(x_ref, g_ref, w_ref, wdth_ref, wdtl_ref, p_ref, dt_ref, h_scr):
    n = pl.program_id(1)

    @pl.when(n == 0)
    def _():
        h = _rms(x_ref[...], g_ref[...])
        h_hi = h.astype(BF16)
        h_lo = (h - h_hi.astype(F32)).astype(BF16)
        h_scr[...] = h_hi
        dt_ref[...] = (jnp.dot(h_hi, wdth_ref[...], preferred_element_type=F32)
                       + jnp.dot(h_hi, wdtl_ref[...], preferred_element_type=F32)
                       + jnp.dot(h_lo, wdth_ref[...], preferred_element_type=F32))

    p_ref[...] = jnp.dot(h_scr[...], w_ref[...], preferred_element_type=F32)


def _inproj(x, g, w_main, w_dt_hi, w_dt_lo):
    t = x.shape[0]
    tm = 1024 if t % 1024 == 0 else min(512, t)
    tn = 2048
    cols = w_main.shape[1]
    return pl.pallas_call(
        _inproj_kernel,
        grid=(t // tm, cols // tn),
        in_specs=[
            pl.BlockSpec((tm, D_MODEL), lambda i, n: (i, 0)),
            pl.BlockSpec((1, D_MODEL), lambda i, n: (0, 0)),
            pl.BlockSpec((D_MODEL, tn), lambda i, n: (0, n)),
            pl.BlockSpec((D_MODEL, LANES), lambda i, n: (0, 0)),
            pl.BlockSpec((D_MODEL, LANES), lambda i, n: (0, 0)),
        ],
        out_specs=[
            pl.BlockSpec((tm, tn), lambda i, n: (i, n)),
            pl.BlockSpec((tm, LANES), lambda i, n: (i, 0)),
        ],
        out_shape=[jax.ShapeDtypeStruct((t, cols), F32), jax.ShapeDtypeStruct((t, LANES), F32)],
        scratch_shapes=[pltpu.VMEM((tm, D_MODEL), BF16)],
        compiler_params=_params(("parallel", "arbitrary")),
        name="inproj",
    )(x, g, w_main, w_dt_hi, w_dt_lo)


_HIST = SUBLANES - (CONV_W - 1)


def _ssd_kernel(*refs, q_real, q_pad, nc, has_init, n_prev):
    prev_refs = ()
    if has_init:
        (xbc_ref, z_ref, dt_ref, h0_ref, cbuf_ref, cw_ref, cb_ref, dtb_ref, alog_ref, dsk_ref,
         ng_ref, e_ref) = refs[:12]
        prev_refs = refs[12:12 + n_prev]
        y_ref, hout_ref, ext_scr, h_scr, dt_scr, z_scr = refs[12 + n_prev:]
    else:
        (xbc_ref, x_ref, dt_ref, cw_ref, cb_ref, dtb_ref, alog_ref, dsk_ref,
         ng_ref, e_ref, gm_ref, wz_ref, y_ref, hout_ref, ext_scr, h_scr, dt_scr) = refs
    c = pl.program_id(1)

    @pl.when(c == 0)
    def _():
        ext_scr[...] = jnp.zeros_like(ext_scr)
        dt_scr[...] = jnp.zeros_like(dt_scr)
        if has_init:
            z_scr[...] = jnp.zeros_like(z_scr)
        if has_init:
            h_scr[...] = h0_ref[0, 0]
            ext_scr[_HIST:SUBLANES, :] = cbuf_ref[0, 0]
        else:
            h_scr[...] = jnp.zeros_like(h_scr)

    ext_scr[SUBLANES:SUBLANES + q_real, :] = xbc_ref[0]
    ext = ext_scr[...]
    acc = cb_ref[...] + cw_ref[CONV_W - 1:CONV_W, :] * ext[SUBLANES:SUBLANES + q_pad, :]
    for w in range(CONV_W - 1):
        acc = acc + cw_ref[w:w + 1, :] * pltpu.roll(ext, CONV_W - 1 - w, axis=0)[SUBLANES:SUBLANES + q_pad, :]
    if nc > 1:
        ext_scr[_HIST:SUBLANES, :] = ext_scr[SUBLANES + q_real - (CONV_W - 1):SUBLANES + q_real, :]
    u = _silu(acc)
    xs = u[:, :D_INNER]
    b_bf = u[:, D_INNER:D_INNER + SSD_GROUPS * D_STATE].astype(BF16)
    c_bf = u[:, D_INNER + SSD_GROUPS * D_STATE:].astype(BF16)

    dt_scr[0:q_real, :] = dt_ref[0]
    if has_init:
        z_scr[0:q_real, :] = z_ref[0]
    else:
        h_bf = _rms(x_ref[0], gm_ref[...]).astype(BF16)
    rows =lax.broadcasted_iota(jnp.int32, (q_pad, LANES), 0)
    dtr = dt_scr[...] + dtb_ref[...]
    softplus = jnp.maximum(dtr, 0.0) + jnp.log1p(jnp.exp(-jnp.abs(dtr)))
    dt = jnp.where(rows < q_real, softplus, 0.0)
    da = dt * (-jnp.exp(alog_ref[...]))

    ri = lax.broadcasted_iota(jnp.int32, (q_pad, q_pad), 0)
    ci = lax.broadcasted_iota(jnp.int32, (q_pad, q_pad), 1)
    causal = ri >= ci
    a_cum = _sel_left(causal.astype(BF16), da)
    a_last = a_cum[q_pad - 1:q_pad, :]
    if q_pad < LANES:
        a_sq = jnp.concatenate([a_cum, jnp.zeros((LANES - q_pad, LANES), F32)], axis=0)
    else:
        a_sq = a_cum
    a_cum_t = a_sq.T

    e01 = e_ref[...]
    dt_x = _sel_right(dt, e01, pieces=2)
    eac_x = _sel_right(jnp.exp(a_cum), e01, pieces=2)
    dte_x = _sel_right(jnp.exp(a_last - a_cum), e01, pieces=2)
    xw = xs * dt_x
    xw_bf = xw.astype(BF16)
    xwd_bf = (xw * dte_x).astype(BF16)

    dec_rows = jnp.broadcast_to(jnp.exp(a_cum_t[:, q_pad - 1:q_pad]), (LANES, LANES))

    lane = lax.broadcasted_iota(jnp.int32, (q_pad, LANES), 1)
    for g in range(SSD_GROUPS):
        cg = c_bf[:, g * D_STATE:(g + 1) * D_STATE]
        bg = b_bf[:, g * D_STATE:(g + 1) * D_STATE]
        cb = lax.dot_general(cg, bg, _NT, preferred_element_type=F32)
        hg = h_scr[g * GROUP_W:(g + 1) * GROUP_W, :]
        y_off = lax.dot_general(cg, hg.astype(BF16), _NT, preferred_element_type=F32)
        parts = []
        for k in range(4):
            slab = g * 4 + k
            xwp = xw_bf[:, slab * LANES:(slab + 1) * LANES]
            pair = []
            for h in (2 * slab, 2 * slab + 1):
                seg = a_cum[:, h:h + 1] - a_cum_t[h:h + 1, :q_pad]
                decay_in = jnp.exp(jnp.where(causal, seg, -jnp.inf))
                pair.append(jnp.dot((cb * decay_in).astype(BF16), xwp, preferred_element_type=F32))
            parts.append(jnp.where(lane < SSD_HEADDIM, pair[0], pair[1]))
        y_diag = jnp.concatenate(parts, axis=1)
        s_new = lax.dot_general(xwd_bf[:, g * GROUP_W:(g + 1) * GROUP_W], bg, _TN,
                                preferred_element_type=F32)
        for hh in range(GROUP_W // SSD_HEADDIM):
            head = g * (GROUP_W // SSD_HEADDIM) + hh
            r0, r1 = hh * SSD_HEADDIM, (hh + 1) * SSD_HEADDIM
            h_scr[g * GROUP_W + r0:g * GROUP_W + r1, :] = (hg[r0:r1, :] * dec_rows[head:head + 1, :]
                                                          + s_new[r0:r1, :])

        gw = slice(g * GROUP_W, (g + 1) * GROUP_W)
        y = y_diag + y_off * eac_x[:, gw] + dsk_ref[:, gw] * xs[:, gw]
        if has_init:
            zg = z_scr[:, gw]
        else:
            zg = jnp.dot(h_bf, wz_ref[:, gw], preferred_element_type=F32)
        ug = y * _silu(zg)
        out = ug * lax.rsqrt(jnp.mean(ug * ug, axis=-1, keepdims=True) + EPS) * ng_ref[:, gw]
        y_ref[0, :, gw] = out[0:q_real, :]

    @pl.when(c == nc - 1)
    def _():
        for k in range(n_prev):
            hout_ref[k, 0] = prev_refs[k][0]
        hout_ref[n_prev, 0] = h_scr[...]


def _ssd(p3, dt3, consts, x3=None, gate_w=None, state=None, collect=()):
    b, L, _ = p3.shape
    q_real = min(SSD_CHUNK, L)
    q_pad = -(-q_real // SUBLANES) * SUBLANES
    nc = L // q_real
    has_init = state is not None
    assert has_init or q_real == q_pad
    cw, cbias, dtb, alog, dsk, ng, e01 = consts

    def const(shape):
        return pl.BlockSpec(shape, lambda i, c: (0,) * len(shape))

    xbc_spec = pl.BlockSpec((1, q_real, CONV_DIM), lambda i, c: (i, c, COL_XBC // CONV_DIM))
    dt_spec = pl.BlockSpec((1, q_real, LANES), lambda i, c: (i, c, 0))
    shared = [const((CONV_W, CONV_DIM)), const((1, CONV_DIM)), const((1, LANES)), const((1, LANES)),
              const((1, D_INNER)), const((1, D_INNER)), const((LANES, D_INNER))]
    scratch = [pltpu.VMEM((SUBLANES + q_pad, CONV_DIM), F32), pltpu.VMEM((D_INNER, D_STATE), F32),
               pltpu.VMEM((q_pad, LANES), F32)]
    if has_init:
        h0, cbuf, layer = state
        in_specs = [xbc_spec, pl.BlockSpec((1, q_real, D_INNER), lambda i, c: (i, c, COL_Z_S // D_INNER)), dt_spec,
                    pl.BlockSpec((1, 1, D_INNER, D_STATE), lambda i, c: (layer, i, 0, 0)),
                    pl.BlockSpec((1, 1, CONV_W - 1, CONV_DIM), lambda i, c: (layer, i, 0, 0))] + shared
        in_specs += [pl.BlockSpec((1, D_INNER, D_STATE), lambda i, c: (i, 0, 0)) for _ in collect]
        args = [p3, p3, dt3, h0, cbuf, cw, cbias, dtb, alog, dsk, ng, e01, *collect]
        scratch.append(pltpu.VMEM((q_pad, D_INNER), F32))
    else:
        assert not collect
        gm, wz = gate_w
        in_specs = ([xbc_spec, pl.BlockSpec((1, q_real, D_MODEL), lambda i, c: (i, c, 0)), dt_spec] + shared
                    + [const((1, D_MODEL)), const((D_MODEL, D_INNER))])
        args = [p3, x3, dt3, cw, cbias, dtb, alog, dsk, ng, e01, gm, wz]
    n_out = len(collect) + 1
    return pl.pallas_call(
        functools.partial(_ssd_kernel, q_real=q_real, q_pad=q_pad, nc=nc, has_init=has_init,
                          n_prev=len(collect)),
        grid=(b, nc),
        in_specs=in_specs,
        out_specs=[
            pl.BlockSpec((1, q_real, D_INNER), lambda i, c: (i, c, 0)),
            pl.BlockSpec((n_out, 1, D_INNER, D_STATE), lambda i, c: (0, i, 0, 0)),
        ],
        out_shape=[jax.ShapeDtypeStruct((b, L, D_INNER), F32),
                   jax.ShapeDtypeStruct((n_out, b, D_INNER, D_STATE), F32)],
        scratch_shapes=scratch,
        compiler_params=_params(("parallel", "arbitrary")),
        name="ssd_s" if has_init else "ssd_p",
    )(*args)


MOBA_P_HEADS = 8
MOBA_P_VMEM = 58 * 1024 * 1024
GATE_ROWS = 2 * SUBLANES


def _moba_p_kernel(slopes_ref, q_ref, k_ref, v_ref, o_ref, kmh_scr, kml_scr, k_scr, vt_scr, bias_scr,
                   sel_scr, qs_scr, acc_scr, m_scr, l_scr, *, nb):
    hg = pl.program_id(1)
    j = pl.program_id(2)
    blk = MOBA_BLOCK
    G = MOBA_P_HEADS
    hd = ATT_HEAD_DIM

    @pl.when(j == 0)
    def _():
        krow = lax.broadcasted_iota(jnp.int32, (blk, blk), 0).astype(F32)
        for g in range(G):
            means = []
            for n in range(nb):
                kn = k_ref[0, n * blk:(n + 1) * blk, g * hd:(g + 1) * hd]
                means.append(jnp.sum(kn, axis=0, keepdims=True) * (1.0 / blk))
                k_scr[n, g] = kn.astype(BF16)
                vt_scr[n, g] = v_ref[0, n * blk:(n + 1) * blk, g * hd:(g + 1) * hd].T.astype(BF16)
            km = jnp.concatenate(means + [jnp.zeros((GATE_ROWS - nb, hd), F32)], axis=0)
            km_hi = km.astype(BF16)
            kmh_scr[g] = km_hi
            kml_scr[g] = (km - km_hi.astype(F32)).astype(BF16)
            bias_scr[g] = slopes_ref[hg * G + g] * krow

    rowid = lax.broadcasted_iota(jnp.int32, (SUBLANES, blk), 0)
    krow_i = lax.broadcasted_iota(jnp.int32, (blk, blk), 0)
    qcol_i = lax.broadcasted_iota(jnp.int32, (blk, blk), 1)
    causal = krow_i <= qcol_i
    scale = hd ** -0.5
    c_own = (j * blk).astype(F32)

    qs_all, gates, s_own = [], [], []
    for g in range(G):
        q = q_ref[0, :, g * hd:(g + 1) * hd]
        q_hi = q.astype(BF16)
        q_lo = (q - q_hi.astype(F32)).astype(BF16)
        qs = (q * scale).astype(BF16)
        qs_all.append(qs)
        gates.append((lax.dot_general(kmh_scr[g], q_hi, _NT, preferred_element_type=F32)
                      + lax.dot_general(kmh_scr[g], q_lo, _NT, preferred_element_type=F32)
                      + lax.dot_general(kml_scr[g], q_hi, _NT, preferred_element_type=F32))[0:SUBLANES, :])
        s_own.append(lax.dot_general(k_scr[j, g], qs, _NT, preferred_element_type=F32))
    sels, m0s, l0s, p_own = [], [], [], []
    for g in range(G):
        gate = jnp.where(rowid < j, gates[g], -jnp.inf)
        cnt = jnp.zeros((SUBLANES, blk), F32)
        for m in range(nb):
            gm = gate[m:m + 1, :]
            beats = (gm > gate) | ((gm == gate) & (m < rowid))
            cnt = cnt + jnp.where(beats, 1.0, 0.0)
        sels.append(jnp.where((cnt < MOBA_TOPK) & (rowid < j), 1.0, 0.0))
        s = jnp.where(causal, s_own[g] + bias_scr[g], -jnp.inf)
        smax = jnp.max(s, axis=0, keepdims=True)
        p = jnp.exp(s - smax)
        m0s.append(smax + slopes_ref[hg * G + g] * jnp.full((1, blk), c_own, F32))
        l0s.append(jnp.sum(p, axis=0, keepdims=True))
        p_own.append(p.astype(BF16))
    for g in range(G):
        acc_scr[g] = jnp.dot(vt_scr[j, g], p_own[g], preferred_element_type=F32)
        sel_scr[g], qs_scr[g], m_scr[g], l_scr[g] = sels[g], qs_all[g], m0s[g], l0s[g]

    def body(n, carry):
        c_n = jnp.full((1, blk), (n * blk).astype(F32), F32)
        scores = [lax.dot_general(k_scr[n, g], qs_scr[g], _NT, preferred_element_type=F32) for g in range(G)]
        stats, probs = [], []
        for g in range(G):
            c_blk = slopes_ref[hg * G + g] * c_n
            s = scores[g] + bias_scr[g]
            m_i = m_scr[g]
            m_use = jnp.maximum(m_i, jnp.max(s, axis=0, keepdims=True) + c_blk)
            p = jnp.exp(s - (m_use - c_blk))
            psum = jnp.sum(p, axis=0, keepdims=True)
            chosen = sel_scr[g, pl.ds(n, 1), :] > 0.0
            alpha = jnp.where(chosen, jnp.exp(m_i - m_use), 1.0)
            keep = jnp.where(chosen, 1.0, 0.0)
            stats.append((jnp.where(chosen, m_use, m_i), alpha * l_scr[g] + keep * psum, alpha, keep))
            probs.append(p.astype(BF16))
        for g in range(G):
            pv = jnp.dot(vt_scr[n, g], probs[g], preferred_element_type=F32)
            m_new, l_new, alpha, keep = stats[g]
            acc_scr[g] = alpha * acc_scr[g] + keep * pv
            m_scr[g], l_scr[g] = m_new, l_new
        return carry

    lax.fori_loop(0, j, body, 0)
    for g in range(G):
        o_ref[0, :, g * hd:(g + 1) * hd] = (acc_scr[g] / l_scr[g]).T


def _moba_p(p3, slopes):
    b, L, _ = p3.shape
    nb = L // MOBA_BLOCK
    G = MOBA_P_HEADS
    gw = G * ATT_HEAD_DIM
    assert nb <= SUBLANES and ATT_HEADS % G == 0 and COL_Q % gw == 0 and COL_K % gw == 0 and COL_V % gw == 0
    cq, ck, cv = COL_Q // gw, COL_K // gw, COL_V // gw
    return pl.pallas_call(
        functools.partial(_moba_p_kernel, nb=nb),
        grid=(b, ATT_HEADS // G, nb),
        in_specs=[
            pl.BlockSpec(memory_space=pltpu.SMEM),
            pl.BlockSpec((1, MOBA_BLOCK, gw), lambda i, h, j: (i, j, cq + h)),
            pl.BlockSpec((1, L, gw), lambda i, h, j: (i, 0, ck + h)),
            pl.BlockSpec((1, L, gw), lambda i, h, j: (i, 0, cv + h)),
        ],
        out_specs=pl.BlockSpec((1, MOBA_BLOCK, gw), lambda i, h, j: (i, j, h)),
        out_shape=jax.ShapeDtypeStruct((b, L, ATT_WIDTH), F32),
        scratch_shapes=[
            pltpu.VMEM((G, GATE_ROWS, ATT_HEAD_DIM), BF16),
            pltpu.VMEM((G, GATE_ROWS, ATT_HEAD_DIM), BF16),
            pltpu.VMEM((nb, G, MOBA_BLOCK, ATT_HEAD_DIM), BF16),
            pltpu.VMEM((nb, G, ATT_HEAD_DIM, MOBA_BLOCK), BF16),
            pltpu.VMEM((G, MOBA_BLOCK, MOBA_BLOCK), F32),
            pltpu.VMEM((G, SUBLANES, MOBA_BLOCK), F32),
            pltpu.VMEM((G, MOBA_BLOCK, ATT_HEAD_DIM), BF16),
            pltpu.VMEM((G, ATT_HEAD_DIM, MOBA_BLOCK), F32),
            pltpu.VMEM((G, 1, MOBA_BLOCK), F32),
            pltpu.VMEM((G, 1, MOBA_BLOCK), F32),
        ],
        compiler_params=_params(("parallel", "parallel", "arbitrary"), MOBA_P_VMEM),
        name="moba_p",
    )(slopes, p3, p3, p3)


MOBA_S_BLOCKS = 4


def _moba_s_kernel(pt_ref, q_ref, kn_ref, vn_ref, *rest, nb, lq, past_len):
    npg = 2 * MOBA_S_BLOCKS
    k_refs, v_refs = rest[:npg], rest[npg:2 * npg]
    o_ref, acc_scr, m_scr, l_scr, g_scr = rest[2 * npg:]
    step = pl.program_id(1)
    rows = lq * ATT_HEADS
    keys = PAGE_SIZE * ATT_HEADS
    rid = lax.broadcasted_iota(jnp.int32, (rows, 1), 0)
    head = rid % ATT_HEADS
    t_pos = (past_len + rid // ATT_HEADS).astype(F32)
    slope = jnp.exp2(-(head + 1).astype(F32) * (8.0 / ATT_HEADS))
    scale = ATT_HEAD_DIM ** -0.5

    q = q_ref[0]
    q_bf = (q * scale).astype(BF16)
    col = lax.broadcasted_iota(jnp.int32, (rows, keys), 1)
    in_page = jnp.where((col % ATT_HEADS) == head, slope * ((col // ATT_HEADS).astype(F32) - t_pos), -jnp.inf)

    raw, gates = [], []
    for i in range(MOBA_S_BLOCKS):
        k0 = k_refs[2 * i][0, 0]
        k1 = k_refs[2 * i + 1][0, 0]
        kmean = (jnp.sum(k0, axis=0) + jnp.sum(k1, axis=0)) * (1.0 / MOBA_BLOCK)
        gates.append(jnp.sum(q * jnp.concatenate([kmean] * lq, axis=0), axis=-1, keepdims=True))
        raw.append([lax.dot_general(q_bf, kh.reshape(keys, ATT_HEAD_DIM).astype(BF16), _NT,
                                    preferred_element_type=F32) for kh in (k0, k1)])
    stats, probs = [], []
    for i in range(MOBA_S_BLOCKS):
        n = step * MOBA_S_BLOCKS + i
        s = [raw[i][h] + in_page + slope * (n * MOBA_BLOCK + h * PAGE_SIZE).astype(F32) for h in range(2)]
        m_n = jnp.maximum(jnp.max(s[0], axis=-1, keepdims=True), jnp.max(s[1], axis=-1, keepdims=True))
        p = [jnp.exp(s[h] - m_n) for h in range(2)]
        l_n = jnp.sum(p[0], axis=-1, keepdims=True) + jnp.sum(p[1], axis=-1, keepdims=True)
        stats.append((m_n, l_n))
        probs.append([ph.astype(BF16) for ph in p])
    for i in range(MOBA_S_BLOCKS):
        n = step * MOBA_S_BLOCKS + i
        acc_scr[n] = sum(jnp.dot(probs[i][h], v_refs[2 * i + h][0, 0].reshape(keys, ATT_HEAD_DIM).astype(BF16),
                                 preferred_element_type=F32) for h in range(2))
        m_scr[n] = jnp.broadcast_to(stats[i][0], (rows, LANES))
        l_scr[n] = jnp.broadcast_to(stats[i][1], (rows, LANES))
        g_scr[n] = jnp.broadcast_to(gates[i], (rows, LANES))

    @pl.when(step == nb // MOBA_S_BLOCKS - 1)
    def _():
        qidx = rid // ATT_HEADS
        s_own = []
        for kk in range(lq):
            kn = jnp.concatenate([kn_ref[0, kk * ATT_HEADS:(kk + 1) * ATT_HEADS, :]] * lq, axis=0)
            s = jnp.sum(q * kn, axis=-1, keepdims=True) * scale
            s = s - slope * (t_pos - float(past_len + kk))
            s_own.append(jnp.where(kk <= qidx, s, -jnp.inf))
        gates = [g_scr[i][:, 0:1] for i in range(nb)]
        ms = [m_scr[i][:, 0:1] for i in range(nb)]
        sel = []
        for i in range(nb):
            cnt = jnp.zeros((rows, 1), F32)
            for m in range(nb):
                beats = (gates[m] > gates[i]) | ((gates[m] == gates[i]) & (m < i))
                cnt = cnt + jnp.where(beats, 1.0, 0.0)
            sel.append(cnt < MOBA_TOPK)
        m_tot = s_own[0]
        for kk in range(1, lq):
            m_tot = jnp.maximum(m_tot, s_own[kk])
        for i in range(nb):
            m_tot = jnp.maximum(m_tot, jnp.where(sel[i], ms[i], -jnp.inf))
        l_tot = jnp.zeros((rows, 1), F32)
        acc = jnp.zeros((rows, ATT_HEAD_DIM), F32)
        for kk in range(lq):
            w = jnp.exp(s_own[kk] - m_tot)
            l_tot = l_tot + w
            acc = acc + w * jnp.concatenate([vn_ref[0, kk * ATT_HEADS:(kk + 1) * ATT_HEADS, :]] * lq, axis=0)
        for i in range(nb):
            w = jnp.where(sel[i], jnp.exp(ms[i] - m_tot), 0.0)
            l_tot = l_tot + w * l_scr[i][:, 0:1]
            acc = acc + w * acc_scr[i]
        o_ref[0] = acc / l_tot


def _moba_s(q2, kn2, vn2, cache_k, cache_v, page_table, layer):
    b, rows, _ = q2.shape
    lq = rows // ATT_HEADS
    n_pages = page_table.shape[1]
    past_len = n_pages * PAGE_SIZE
    nb = past_len // MOBA_BLOCK
    ppb = MOBA_BLOCK // PAGE_SIZE
    assert ppb == 2 and nb >= MOBA_TOPK and lq <= MOBA_BLOCK and nb % MOBA_S_BLOCKS == 0
    npg = ppb * MOBA_S_BLOCKS

    def page(off):
        return pl.BlockSpec((1, 1, PAGE_SIZE, ATT_HEADS, ATT_HEAD_DIM),
                            lambda i, n, pt: (layer, pt[i, npg * n + off], 0, 0, 0))

    def new(_):
        return pl.BlockSpec((1, rows, ATT_HEAD_DIM), lambda i, n, pt: (i, 0, 0))

    pages = [page(off) for off in range(npg)]
    grid_spec = pltpu.PrefetchScalarGridSpec(
        num_scalar_prefetch=1,
        grid=(b, nb // MOBA_S_BLOCKS),
        in_specs=[new(0), new(1), new(2)] + pages + pages,
        out_specs=pl.BlockSpec((1, rows, ATT_HEAD_DIM), lambda i, n, pt: (i, 0, 0)),
        scratch_shapes=[
            pltpu.VMEM((nb, rows, ATT_HEAD_DIM), F32),
            pltpu.VMEM((nb, rows, LANES), F32),
            pltpu.VMEM((nb, rows, LANES), F32),
            pltpu.VMEM((nb, rows, LANES), F32),
        ],
    )
    return pl.pallas_call(
        functools.partial(_moba_s_kernel, nb=nb, lq=lq, past_len=past_len),
        grid_spec=grid_spec,
        out_shape=jax.ShapeDtypeStruct((b, rows, ATT_HEAD_DIM), F32),
        compiler_params=_params(("parallel", "arbitrary")),
        name="moba_s",
    )(page_table, q2, kn2, vn2, *([cache_k] * npg), *([cache_v] * npg))


def _merge_kernel(x_ref, ya_ref, yb_ref, ga_ref, gb_ref, wa_ref, wb_ref, wo_ref, o_ref):
    a = jnp.dot(ya_ref[...].astype(BF16), wa_ref[...], preferred_element_type=F32)
    b = jnp.dot(yb_ref[...].astype(BF16), wb_ref[...], preferred_element_type=F32)
    m = jax.nn.sigmoid(ga_ref[...]) * a + jax.nn.sigmoid(gb_ref[...]) * b
    o_ref[...] = x_ref[...] + jnp.dot(m.astype(BF16), wo_ref[...], preferred_element_type=F32)


def _merge(x, ya, yb, p, wpa, wpb, wo):
    t = x.shape[0]
    tm = min(512, t)
    return pl.pallas_call(
        _merge_kernel,
        grid=(t // tm,),
        in_specs=[
            pl.BlockSpec((tm, D_MODEL), lambda i: (i, 0)),
            pl.BlockSpec((tm, D_INNER), lambda i: (i, 0)),
            pl.BlockSpec((tm, ATT_WIDTH), lambda i: (i, 0)),
            pl.BlockSpec((tm, D_MODEL), lambda i: (i, COL_GA // D_MODEL)),
            pl.BlockSpec((tm, D_MODEL), lambda i: (i, COL_GB // D_MODEL)),
            pl.BlockSpec((D_INNER, D_MODEL), lambda i: (0, 0)),
            pl.BlockSpec((ATT_WIDTH, D_MODEL), lambda i: (0, 0)),
            pl.BlockSpec((D_MODEL, D_MODEL), lambda i: (0, 0)),
        ],
        out_specs=pl.BlockSpec((tm, D_MODEL), lambda i: (i, 0)),
        out_shape=jax.ShapeDtypeStruct((t, D_MODEL), F32),
        compiler_params=_params(("parallel",)),
        name="merge",
    )(x, ya, yb, p, p, wpa, wpb, wo)


def _kvout_kernel(*refs, depth):
    ok_ref, ov_ref = refs[2 * depth:]
    for l in range(depth):
        ok_ref[l] = refs[2 * l][...].reshape(ok_ref.shape[1:])
        ov_ref[l] = refs[2 * l + 1][...].reshape(ov_ref.shape[1:])


def _kvout(ps):
    depth = len(ps)
    t = ps[0].shape[0]
    tm = min(512, t)
    in_specs, args = [], []
    for p in ps:
        in_specs += [pl.BlockSpec((tm, ATT_WIDTH), lambda i: (i, COL_K // ATT_WIDTH)),
                     pl.BlockSpec((tm, ATT_WIDTH), lambda i: (i, COL_V // ATT_WIDTH))]
        args += [p, p]
    out_spec = pl.BlockSpec((depth, tm, ATT_HEADS, ATT_HEAD_DIM), lambda i: (0, i, 0, 0))
    out_shape = jax.ShapeDtypeStruct((depth, t, ATT_HEADS, ATT_HEAD_DIM), F32)
    return pl.pallas_call(
        functools.partial(_kvout_kernel, depth=depth),
        grid=(t // tm,),
        in_specs=in_specs,
        out_specs=[out_spec, out_spec],
        out_shape=[out_shape, out_shape],
        compiler_params=_params(("parallel",)),
        name="kvout",
    )(*args)


def _pad_lanes(v):
    return jnp.pad(v.astype(F32), (0, LANES - v.shape[0])).reshape(1, LANES)


def kernel(x_prompt, x_sample, cache_k, cache_v, state_ssm, state_conv, page_table, ln_ffn1, w_ffn1_in,
           w_ffn1_out, ln_mix, w_in, conv_w, conv_b, dt_bias, a_log, d_skip, ssd_norm, w_proj_a, w_proj_b,
           w_out, ln_ffn2, w_ffn2_in, w_ffn2_out, ln_final):
    depth = w_in.shape[0]
    bp, seq, _ = x_prompt.shape
    bs, lq, _ = x_sample.shape
    slopes = jnp.exp2(-8.0 * jnp.arange(1, ATT_HEADS + 1, dtype=F32) / ATT_HEADS)
    head_of = jnp.arange(D_INNER, dtype=jnp.int32) // SSD_HEADDIM
    e01 = (jnp.arange(LANES, dtype=jnp.int32)[:, None] == head_of[None, :]).astype(BF16)
    h0_all = state_ssm.reshape(depth, bs, D_INNER, D_STATE)

    o_z, o_xbc, o_dt = 0, D_INNER, D_INNER + CONV_DIM
    o_q = o_dt + SSD_HEADS

    xp = x_prompt.reshape(bp * seq, D_MODEL)
    xs = x_sample.reshape(bs * lq, D_MODEL)
    outs = {k: [] for k in ("hp", "cp", "ks", "vs", "hs", "cs")}
    pps = []
    row = lambda v: v.astype(F32).reshape(1, -1)
    heads_rows = lambda a: a.reshape(bs, lq * ATT_HEADS, ATT_HEAD_DIM)
    for l in range(depth):
        wl = w_in[l]
        w_main = jnp.concatenate([wl[:, o_xbc:o_dt], wl[:, o_q:]], axis=1).astype(BF16)
        w_z = wl[:, o_z:o_xbc].astype(BF16)
        w_dt = jnp.pad(wl[:, o_dt:o_q], ((0, 0), (0, LANES - SSD_HEADS)))
        w_dt_hi = w_dt.astype(BF16)
        w_dt_lo = (w_dt - w_dt_hi.astype(F32)).astype(BF16)
        w1i, w1o = w_ffn1_in[l].astype(BF16), w_ffn1_out[l].astype(BF16)
        w2i, w2o = w_ffn2_in[l].astype(BF16), w_ffn2_out[l].astype(BF16)
        wpa, wpb, wo = w_proj_a[l].astype(BF16), w_proj_b[l].astype(BF16), w_out[l].astype(BF16)
        consts = (conv_w[l], row(conv_b[l]), _pad_lanes(dt_bias[l]), _pad_lanes(a_log[l]),
                  row(jnp.repeat(d_skip[l], SSD_HEADDIM)), row(ssd_norm[l]), e01)
        g1, gm, g2 = row(ln_ffn1[l]), row(ln_mix[l]), row(ln_ffn2[l])

        xp = _ffn(xp, g1, w1i, w1o)
        xs = _ffn(xs, g1, w1i, w1o)
        pp, dtp = _inproj(xp, gm, w_main, w_dt_hi, w_dt_lo)
        ps, dts = _inproj(xs, gm, jnp.concatenate([w_main, w_z], axis=1), w_dt_hi, w_dt_lo)
        pp3 = pp.reshape(bp, seq, P_COLS)
        ps3 = ps.reshape(bs, lq, COL_Z_S + D_INNER)
        pps.append(pp)
        ks_new = ps3[:, :, COL_K:COL_K + ATT_WIDTH].reshape(bs, lq, ATT_HEADS, ATT_HEAD_DIM)
        vs_new = ps3[:, :, COL_V:COL_V + ATT_WIDTH].reshape(bs, lq, ATT_HEADS, ATT_HEAD_DIM)

        ya_p, h_p = _ssd(pp3, dtp.reshape(bp, seq, LANES), consts, x3=xp.reshape(bp, seq, D_MODEL),
                         gate_w=(gm, w_z))
        ya_s, h_s = _ssd(ps3, dts.reshape(bs, lq, LANES), consts, state=(h0_all, state_conv, l),
                         collect=tuple(outs["hs"]) if l == depth - 1 else ())
        yb_p = _moba_p(pp3, slopes)
        yb_s = _moba_s(heads_rows(ps3[:, :, COL_Q:COL_Q + ATT_WIDTH]), heads_rows(ks_new), heads_rows(vs_new),
                       cache_k, cache_v, page_table, l)

        xp = _merge(xp, ya_p.reshape(bp * seq, D_INNER), yb_p.reshape(bp * seq, ATT_WIDTH), pp, wpa, wpb, wo)
        xs = _merge(xs, ya_s.reshape(bs * lq, D_INNER), yb_s.reshape(bs * lq, ATT_WIDTH), ps, wpa, wpb, wo)
        g_last = row(ln_final) if l == depth - 1 else None
        xp = _ffn(xp, g2, w2i, w2o, g_last)
        xs = _ffn(xs, g2, w2i, w2o, g_last)

        outs["hp"].append(h_p.reshape(bp, SSD_HEADS, SSD_HEADDIM, D_STATE))
        outs["cp"].append(pp3[:, seq - (CONV_W - 1):, COL_XBC:COL_XBC + CONV_DIM])
        outs["ks"].append(ks_new)
        outs["vs"].append(vs_new)
        if l == depth - 1:
            ssm_sample = h_s.reshape(depth, bs, SSD_HEADS, SSD_HEADDIM, D_STATE)
        else:
            outs["hs"].append(h_s[0])
        if lq >= CONV_W - 1:
            cs = ps3[:, lq - (CONV_W - 1):, COL_XBC:COL_XBC + CONV_DIM]
        else:
            cs = jnp.concatenate([state_conv[l], ps3[:, :, COL_XBC:COL_XBC + CONV_DIM]], axis=1)[:, -(CONV_W - 1):]
        outs["cs"].append(cs)

    y_prompt = xp.reshape(bp, seq, D_MODEL)
    y_sample = xs.reshape(bs, lq, D_MODEL)
    k_prompt, v_prompt = _kvout(pps)
    kv5 = lambda a: a.reshape(depth, bp, seq, ATT_HEADS, ATT_HEAD_DIM)
    st = lambda k: jnp.stack(outs[k])
    return (y_prompt, y_sample, kv5(k_prompt), kv5(v_prompt), st("hp"), st("cp"), st("ks"), st("vs"), ssm_sample,
            st("cs"))
```
